```python
import math
import jax, jax.numpy as jnp
from jax import lax
import numpy as np

D_MODEL = 2048
BATCH = 4
SEQ = 4096
DEPTH = 1

MIX_WIDTH = D_MODEL
HEAD_DIM = 128
ATTN_WIDTH = MIX_WIDTH // 2
ATTN_HEADS = ATTN_WIDTH // HEAD_DIM
POOL_WIDTH = MIX_WIDTH - ATTN_WIDTH
POOL_WINDOWS = (2, 4, 8, 16)
POOL_GROUPS = len(POOL_WINDOWS)
POOL_GROUP_WIDTH = POOL_WIDTH // POOL_GROUPS
IN_WIDTH = 3 * ATTN_WIDTH + POOL_WIDTH
MOBA_BLOCK = 256
MOBA_TOPK = 3
Q_CHUNK = 64
REL_BUCKETS = 32
REL_MAX_DIST = 1024
D_FF = 256 * ((8 * D_MODEL // 3 + 255) // 256)
CONV_WIDTH = 3
EPS = 1e-6
NEG = -1e30

kernel_name = 'hybrid_moba_pool_convffn'


def rms_norm(x, g):
    xf = x.astype(jnp.float32)
    y = xf * lax.rsqrt(jnp.mean(xf * xf, axis=-1, keepdims=True) + EPS)
    return (y * g.astype(jnp.float32)).astype(x.dtype)


def rel_bucket(dist):
    n = jnp.maximum(dist, 0)
    max_exact = REL_BUCKETS // 2
    nf = jnp.maximum(n, max_exact).astype(jnp.float32)
    large = max_exact + (jnp.log(nf / max_exact) / math.log(REL_MAX_DIST / max_exact)
                         * (REL_BUCKETS - max_exact)).astype(jnp.int32)
    large = jnp.minimum(large, REL_BUCKETS - 1)
    return jnp.where(n < max_exact, n, large)


def moba_attention(q, k, v, rel_bias):
    B, H, S, Dh = q.shape
    nb = -(-S // MOBA_BLOCK)
    S_pad = nb * MOBA_BLOCK
    pad = ((0, 0), (0, 0), (0, S_pad - S), (0, 0))
    q, k, v = jnp.pad(q, pad), jnp.pad(k, pad), jnp.pad(v, pad)
    K = min(MOBA_TOPK, nb)
    kb = k.reshape(B, H, nb, MOBA_BLOCK, Dh)
    vb = v.reshape(B, H, nb, MOBA_BLOCK, Dh)
    k_mean = jnp.mean(kb.astype(jnp.float32), axis=3)
    gate = jnp.einsum('bhsd,bhnd->bhsn', q.astype(jnp.float32), k_mean)
    q_blk = jnp.arange(S_pad) // MOBA_BLOCK
    past = jnp.arange(nb)[None, :] < q_blk[:, None]
    gate = jnp.where(past, gate, NEG)
    _, sel = lax.top_k(gate, K)
    n_chunks = S_pad // Q_CHUNK
    sel_chunks = sel.reshape(B, H, n_chunks, Q_CHUNK, K).transpose(2, 0, 1, 3, 4)
    b_idx = jnp.arange(B)[:, None, None, None]
    h_idx = jnp.arange(H)[None, :, None, None]
    offs = jnp.arange(MOBA_BLOCK)
    scale = Dh ** -0.5

    def chunk(args):
        c, sel_c = args
        q0 = c * Q_CHUNK
        own = q0 // MOBA_BLOCK
        qc = lax.dynamic_slice_in_dim(q, q0, Q_CHUNK, axis=2)
        q_pos = q0 + jnp.arange(Q_CHUNK)
        k_sel = kb[b_idx, h_idx, sel_c]
        v_sel = vb[b_idx, h_idx, sel_c]
        s_sel = jnp.einsum('bhqd,bhqnkd->bhqnk', qc, k_sel).astype(jnp.float32) * scale
        kpos_sel = sel_c[..., None] * MOBA_BLOCK + offs
        bias_sel = rel_bias[h_idx[..., None], rel_bucket(q_pos[:, None, None] - kpos_sel)]
        valid = (sel_c < own)[..., None]
        s_sel = jnp.where(valid, s_sel + bias_sel.astype(jnp.float32), NEG)
        s_sel = s_sel.reshape(B, H, Q_CHUNK, K * MOBA_BLOCK)
        k_own = lax.dynamic_index_in_dim(kb, own, axis=2, keepdims=False)
        v_own = lax.dynamic_index_in_dim(vb, own, axis=2, keepdims=False)
        s_own = jnp.einsum('bhqd,bhkd->bhqk', qc, k_own).astype(jnp.float32) * scale
        dist_own = q_pos[:, None] - (own * MOBA_BLOCK + offs)[None, :]
        bias_own = rel_bias[:, rel_bucket(dist_own)].astype(jnp.float32)
        s_own = jnp.where(dist_own >= 0, s_own + bias_own, NEG)
        probs = jax.nn.softmax(jnp.concatenate([s_sel, s_own], axis=-1), axis=-1)
        p_sel = probs[..., :K * MOBA_BLOCK].reshape(
            B, H, Q_CHUNK, K, MOBA_BLOCK).astype(v.dtype)
        p_own = probs[..., K * MOBA_BLOCK:].astype(v.dtype)
        return (jnp.einsum('bhqnk,bhqnkd->bhqd', p_sel, v_sel)
                + jnp.einsum('bhqk,bhkd->bhqd', p_own, v_own))

    outs = lax.map(chunk, (jnp.arange(n_chunks), sel_chunks))
    out = outs.transpose(1, 0, 3, 2, 4).reshape(B, S_pad, H * Dh)
    return out[:, :S]


def multiscale_pool(p, pool_w, pool_scale):
    B, S, C = p.shape
    pf = p.astype(jnp.float32)
    csum = jnp.concatenate([jnp.zeros((B, 1, C), jnp.float32), jnp.cumsum(pf, axis=1)], axis=1)
    t = jnp.arange(S)
    outs = []
    for g, w in enumerate(POOL_WINDOWS):
        sl = slice(g * POOL_GROUP_WIDTH, (g + 1) * POOL_GROUP_WIDTH)
        cg = csum[:, :, sl]
        start = jnp.maximum(t + 1 - w, 0)
        count = jnp.minimum(t + 1, w).astype(jnp.float32)[None, :, None]
        mixed = (cg[:, 1:] - cg[:, start]) / count - pf[:, :, sl]
        outs.append(jnp.einsum('bsc,cd->bsd', mixed.astype(p.dtype), pool_w[g]))
    return jnp.concatenate(outs, axis=-1) * pool_scale


def conv_ffn(h, w_up, conv_w, conv_b, w_down):
    u = h @ w_up
    S = u.shape[1]
    u_pad = jnp.pad(u, ((0, 0), (CONV_WIDTH - 1, 0), (0, 0)))
    y = conv_b
    for i in range(CONV_WIDTH):
        y = y + conv_w[i] * u_pad[:, i:i + S]
    gate, val = jnp.split(y, 2, axis=-1)
    return (jax.nn.silu(gate) * val) @ w_down


def setup_inputs(seed: int = 0) -> dict:
    key = jax.random.key(seed)
    ks = jax.random.split(key, 16)
    f32 = jnp.float32
    nrm = lambda k, shape, s: jax.random.normal(k, shape, f32) * s
    return {
        'x': nrm(ks[0], (BATCH, SEQ, D_MODEL), 1.0),
        'attn_norm_g': 1.0 + nrm(ks[1], (DEPTH, D_MODEL), 0.02),
        'w_in': nrm(ks[2], (DEPTH, D_MODEL, IN_WIDTH), D_MODEL ** -0.5),
        'q_norm_g': 1.0 + nrm(ks[3], (DEPTH, HEAD_DIM), 0.02),
        'k_norm_g': 1.0 + nrm(ks[4], (DEPTH, HEAD_DIM), 0.02),
        'rel_bias': nrm(ks[5], (ATTN_HEADS, REL_BUCKETS), 0.5),
        'pool_w': nrm(ks[6], (DEPTH, POOL_GROUPS, POOL_GROUP_WIDTH, POOL_GROUP_WIDTH), POOL_GROUP_WIDTH ** -0.5),
        'pool_scale': 1.0 + nrm(ks[7], (DEPTH, POOL_WIDTH), 0.1),
        'w_out': nrm(ks[8], (DEPTH, MIX_WIDTH, D_MODEL), MIX_WIDTH ** -0.5),
        'ffn_norm_g': 1.0 + nrm(ks[9], (DEPTH, D_MODEL), 0.02),
        'w_up': nrm(ks[10], (DEPTH, D_MODEL, 2 * D_FF), D_MODEL ** -0.5),
        'conv_w': nrm(ks[11], (DEPTH, CONV_WIDTH, 2 * D_FF), CONV_WIDTH ** -0.5),
        'conv_b': nrm(ks[12], (DEPTH, 2 * D_FF), 0.01),
        'w_down': nrm(ks[13], (DEPTH, D_FF, D_MODEL), D_FF ** -0.5),
    }


def reference(x, attn_norm_g, w_in, q_norm_g, k_norm_g, rel_bias, pool_w, pool_scale,
              w_out, ffn_norm_g, w_up, conv_w, conv_b, w_down):
    B, S, _ = x.shape

    def heads(t):
        return t.reshape(B, S, ATTN_HEADS, HEAD_DIM).transpose(0, 2, 1, 3)

    for l in range(DEPTH):
        h = rms_norm(x, attn_norm_g[l])
        proj = h @ w_in[l]
        q, k, v, p = jnp.split(proj, [ATTN_WIDTH, 2 * ATTN_WIDTH, 3 * ATTN_WIDTH], axis=-1)
        q = rms_norm(heads(q), q_norm_g[l])
        k = rms_norm(heads(k), k_norm_g[l])
        a = moba_attention(q, k, heads(v), rel_bias)
        m = multiscale_pool(p, pool_w[l], pool_scale[l])
        x = x + jnp.concatenate([a, m], axis=-1) @ w_out[l]
        x = x + conv_ffn(rms_norm(x, ffn_norm_g[l]), w_up[l], conv_w[l], conv_b[l], w_down[l])
    return x
```

```python
import functools
import math

import jax
import jax.numpy as jnp
from jax import lax
from jax.experimental import pallas as pl
from jax.experimental.pallas import tpu as pltpu

HEAD_DIM = 128
MOBA_BLOCK = 256
MOBA_TOPK = 3
POOL_WINDOWS = (2, 4, 8, 16)
REL_BUCKETS = 32
REL_MAX_DIST = 1024
CONV_WIDTH = 3
EPS = 1e-6
NEG = -1e30

POOL_HALO = 16
CONV_HALO = 8
NEAR_DIST = 5

VMEM_LIMIT = 56 * 1024 * 1024

BF16 = jnp.bfloat16
F32 = jnp.float32


def _rel_bucket_thresholds():
    max_exact = REL_BUCKETS // 2
    span = REL_BUCKETS - max_exact
    out = []
    for k in range(1, span):
        edge = max_exact * (REL_MAX_DIST / max_exact) ** (k / span)
        r = round(edge)
        out.append(r if abs(edge - r) < 1e-9 else math.ceil(edge))
    return tuple(out)


def _bias_kernel(rb_ref, o_ref):
    h = pl.program_id(0)
    d = pl.program_id(1)
    shape = (MOBA_BLOCK, MOBA_BLOCK)
    i = lax.broadcasted_iota(jnp.int32, shape, 0)
    j = lax.broadcasted_iota(jnp.int32, shape, 1)
    n = jnp.maximum(d * MOBA_BLOCK + i - j, 0)
    max_exact = REL_BUCKETS // 2
    large = jnp.full(shape, max_exact, jnp.int32)
    for t in _rel_bucket_thresholds():
        large = large + (n >= t).astype(jnp.int32)
    bucket = jnp.where(n < max_exact, n, large)
    val = jnp.zeros(shape, F32)
    for b in range(REL_BUCKETS):
        val = jnp.where(bucket == b, rb_ref[h, b], val)
    o_ref[0, 0] = val


def _bias_tables(rel_bias):
    heads = rel_bias.shape[0]
    return pl.pallas_call(
        _bias_kernel,
        grid=(heads, NEAR_DIST + 1),
        in_specs=[pl.BlockSpec(memory_space=pltpu.SMEM)],
        out_specs=pl.BlockSpec((1, 1, MOBA_BLOCK, MOBA_BLOCK), lambda h, d: (h, d, 0, 0)),
        out_shape=jax.ShapeDtypeStruct((heads, NEAR_DIST + 1, MOBA_BLOCK, MOBA_BLOCK), F32),
        name="bias_tables",
    )(rel_bias)


def _rms(y, g):
    ms = jnp.mean(y * y, axis=-1, keepdims=True)
    return y * lax.rsqrt(ms + EPS) * g


def _inproj_kernel(x_ref, g_ref, w_ref, qg_ref, kg_ref, pw_ref, ps_ref, o_ref, h_sc, halo_sc,
                   *, tm, tiles_per_seq, heads, group_width):
    i = pl.program_id(0)
    j = pl.program_id(1)

    @pl.when(j == 0)
    def _():
        h_sc[...] = _rms(x_ref[...], g_ref[...]).astype(BF16)

    y = jnp.dot(h_sc[...], w_ref[...], preferred_element_type=F32)

    def head_norm(g):
        for h in range(heads):
            sl = slice(h * HEAD_DIM, (h + 1) * HEAD_DIM)
            o_ref[:, sl] = _rms(y[:, sl], g).astype(BF16)

    @pl.when(j == 0)
    def _():
        head_norm(qg_ref[...] * (HEAD_DIM ** -0.5))

    @pl.when(j == 1)
    def _():
        head_norm(kg_ref[...])

    @pl.when(j == 2)
    def _():
        o_ref[...] = y.astype(BF16)

    @pl.when(j == 3)
    def _():
        seq_tile = i % tiles_per_seq
        halo = jnp.where(seq_tile == 0, 0.0, halo_sc[...])
        halo_sc[...] = y[tm - POOL_HALO:, :]
        t = seq_tile * tm + lax.broadcasted_iota(jnp.int32, (tm, 1), 0)
        for g, w in enumerate(POOL_WINDOWS):
            sl = slice(g * group_width, (g + 1) * group_width)
            pg = y[:, sl]
            a = jnp.concatenate([halo[:, sl], pg], axis=0)
            shift = 1
            while shift < w:
                a = a + pltpu.roll(a, shift, axis=0)
                shift *= 2
            count = jnp.minimum(t + 1, w).astype(F32)
            mixed = a[POOL_HALO:, :] / count - pg
            mg = jnp.dot(mixed.astype(BF16), pw_ref[g], preferred_element_type=F32)
            o_ref[:, sl] = (mg * ps_ref[:, sl]).astype(BF16)


def _in_proj(x2, attn_g, w_in, q_g, k_g, pool_w, pool_scale, *, seq, tm):
    tokens, d_model = x2.shape
    in_width = w_in.shape[1]
    tn = in_width // 4
    heads = tn // HEAD_DIM
    groups, group_width, _ = pool_w.shape
    assert groups == len(POOL_WINDOWS) and groups * group_width == tn
    assert seq % tm == 0 and tm >= POOL_HALO
    kern = functools.partial(_inproj_kernel, tm=tm, tiles_per_seq=seq // tm, heads=heads,
                             group_width=group_width)
    return pl.pallas_call(
        kern,
        grid=(tokens // tm, 4),
        in_specs=[
            pl.BlockSpec((tm, d_model), lambda i, j: (i, 0)),
            pl.BlockSpec((1, d_model), lambda i, j: (0, 0)),
            pl.BlockSpec((d_model, tn), lambda i, j: (0, j)),
            pl.BlockSpec((1, HEAD_DIM), lambda i, j: (0, 0)),
            pl.BlockSpec((1, HEAD_DIM), lambda i, j: (0, 0)),
            pl.BlockSpec((groups, group_width, group_width), lambda i, j: (0, 0, 0)),
            pl.BlockSpec((1, tn), lambda i, j: (0, 0)),
        ],
        out_specs=pl.BlockSpec((tm, tn), lambda i, j: (i, j)),
        out_shape=jax.ShapeDtypeStruct((tokens, in_width), BF16),
        scratch_shapes=[pltpu.VMEM((tm, d_model), BF16), pltpu.VMEM((POOL_HALO, tn), F32)],
        compiler_params=pltpu.CompilerParams(
            dimension_semantics=("arbitrary", "arbitrary"), vmem_limit_bytes=VMEM_LIMIT),
        name="in_proj",
    )(x2, attn_g, w_in, q_g, k_g, pool_w, pool_scale)


def _nt_dot(a, b):
    return lax.dot_general(a, b, (((1,), (1,)), ((), ())), preferred_element_type=F32)


def _attn_kernel(q_ref, k_ref, v_ref, bias_ref, o_ref, kmean_sc, *, nb):
    qb = pl.program_id(2)
    blk = MOBA_BLOCK

    @pl.when(qb == 0)
    def _():
        for n in range(nb):
            kn = k_ref[n * blk:(n + 1) * blk, :].astype(F32)
            kmean_sc[n:n + 1, :] = jnp.mean(kn, axis=0, keepdims=True)

    q = q_ref[...]

    km = kmean_sc[...]
    km_hi = km.astype(BF16)
    km_lo = (km - km_hi.astype(F32)).astype(BF16)
    gate = _nt_dot(q, km_hi) + _nt_dot(q, km_lo)

    n_iota = lax.broadcasted_iota(jnp.int32, (blk, nb), 1)
    past = n_iota < qb
    g = jnp.where(past, gate, NEG)
    rank = jnp.zeros((blk, nb), jnp.int32)
    for m in range(nb):
        gm = g[:, m:m + 1]
        beats = (gm > g) | ((gm == g) & (n_iota > m))
        rank = rank + beats.astype(jnp.int32)
    negmask = jnp.where((rank < MOBA_TOPK) & past, 0.0, NEG)

    start = pl.multiple_of(qb * blk, blk)
    s = _nt_dot(q, k_ref[pl.ds(start, blk), :]) + bias_ref[0, 0]
    row = lax.broadcasted_iota(jnp.int32, (blk, blk), 0)
    col = lax.broadcasted_iota(jnp.int32, (blk, blk), 1)
    s = jnp.where(row >= col, s, NEG)
    m0 = jnp.max(s, axis=1, keepdims=True)
    p = jnp.exp(s - m0)
    l0 = jnp.sum(p, axis=1, keepdims=True)
    acc0 = jnp.dot(p.astype(BF16), v_ref[pl.ds(start, blk), :], preferred_element_type=F32)

    def body(n, carry):
        m_run, l_run, acc = carry
        st = pl.multiple_of(n * blk, blk)
        s = _nt_dot(q, k_ref[pl.ds(st, blk), :])
        s = s + bias_ref[0, jnp.minimum(qb - n, NEAR_DIST)]
        s = s + jnp.sum(jnp.where(n_iota == n, negmask, 0.0), axis=1, keepdims=True)
        m_new = jnp.maximum(m_run, jnp.max(s, axis=1, keepdims=True))
        alpha = jnp.exp(m_run - m_new)
        p = jnp.exp(s - m_new)
        l_new = alpha * l_run + jnp.sum(p, axis=1, keepdims=True)
        acc_new = alpha * acc + jnp.dot(p.astype(BF16), v_ref[pl.ds(st, blk), :],
                                        preferred_element_type=F32)
        return m_new, l_new, acc_new

    _, l_fin, acc_fin = lax.fori_loop(0, qb, body, (m0, l0, acc0))
    o_ref[...] = (acc_fin / l_fin).astype(BF16)


def _moba_attention(qkvm, bias_tab, *, batch, seq, heads):
    tokens = qkvm.shape[0]
    nb = seq // MOBA_BLOCK
    assert seq % MOBA_BLOCK == 0
    kern = functools.partial(_attn_kernel, nb=nb)
    return pl.pallas_call(
        kern,
        grid=(heads, batch, nb),
        in_specs=[
            pl.BlockSpec((MOBA_BLOCK, HEAD_DIM), lambda h, b, qb: (b * nb + qb, h)),
            pl.BlockSpec((seq, HEAD_DIM), lambda h, b, qb: (b, heads + h)),
            pl.BlockSpec((seq, HEAD_DIM), lambda h, b, qb: (b, 2 * heads + h)),
            pl.BlockSpec((1, NEAR_DIST + 1, MOBA_BLOCK, MOBA_BLOCK), lambda h, b, qb: (h, 0, 0, 0)),
        ],
        out_specs=pl.BlockSpec((MOBA_BLOCK, HEAD_DIM), lambda h, b, qb: (b * nb + qb, h)),
        out_shape=jax.ShapeDtypeStruct((tokens, heads * HEAD_DIM), BF16),
        scratch_shapes=[pltpu.VMEM((nb, HEAD_DIM), F32)],
        compiler_params=pltpu.CompilerParams(
            dimension_semantics=("arbitrary", "arbitrary", "arbitrary"), vmem_limit_bytes=VMEM_LIMIT),
        name="moba_attn",
    )(qkvm, qkvm, qkvm, bias_tab)


def _outproj_kernel(x_ref, a_ref, m_ref, wa_ref, wm_ref, o_ref):
    o_ref[...] = (x_ref[...]
                  + jnp.dot(a_ref[...], wa_ref[...], preferred_element_type=F32)
                  + jnp.dot(m_ref[...], wm_ref[...], preferred_element_type=F32))


def _out_proj(x2, a, qkvm, w_out, *, tm):
    tokens, d_model = x2.shape
    half = a.shape[1]
    m_col = qkvm.shape[1] // half - 1
    return pl.pallas_call(
        _outproj_kernel,
        grid=(tokens // tm,),
        in_specs=[
            pl.BlockSpec((tm, d_model), lambda i: (i, 0)),
            pl.BlockSpec((tm, half), lambda i: (i, 0)),
            pl.BlockSpec((tm, half), lambda i: (i, m_col)),
            pl.BlockSpec((half, d_model), lambda i: (0, 0)),
            pl.BlockSpec((half, d_model), lambda i: (1, 0)),
        ],
        out_specs=pl.BlockSpec((tm, d_model), lambda i: (i, 0)),
        out_shape=jax.ShapeDtypeStruct((tokens, d_model), F32),
        compiler_params=pltpu.CompilerParams(
            dimension_semantics=("arbitrary",), vmem_limit_bytes=VMEM_LIMIT),
        name="out_proj",
    )(x2, a, qkvm, w_out, w_out)


def _ffn_kernel(x_ref, g_ref, wug_ref, wuv_ref, cwg_ref, cwv_ref, cbg_ref, cbv_ref, wd_ref,
                o_ref, h_sc, carry_sc, *, tm, tiles_per_seq):
    i = pl.program_id(0)
    f = pl.program_id(1)

    @pl.when(f == 0)
    def _():
        x = x_ref[...]
        h_sc[...] = _rms(x, g_ref[...]).astype(BF16)
        o_ref[...] = x

    first = (i % tiles_per_seq) == 0
    h = h_sc[...]

    def conv(w_ref, cw_ref, cb_ref, slot):
        u = jnp.dot(h, w_ref[...], preferred_element_type=F32)
        prev = jnp.where(first, 0.0, carry_sc[slot, f])
        carry_sc[slot, f] = u[tm - CONV_HALO:, :]
        e = jnp.concatenate([prev, u], axis=0)
        y = cb_ref[...]
        for tap in range(CONV_WIDTH - 1):
            y = y + cw_ref[tap:tap + 1, :] * pltpu.roll(e, CONV_WIDTH - 1 - tap, axis=0)[CONV_HALO:, :]
        return y + cw_ref[CONV_WIDTH - 1:CONV_WIDTH, :] * u

    yg = conv(wug_ref, cwg_ref, cbg_ref, 0)
    yv = conv(wuv_ref, cwv_ref, cbv_ref, 1)
    act = yg / (1.0 + jnp.exp(-yg)) * yv
    o_ref[...] += jnp.dot(act.astype(BF16), wd_ref[...], preferred_element_type=F32)


def _conv_ffn(x1, ffn_g, w_up, conv_w, conv_b, w_down, *, seq, tm, tf):
    tokens, d_model = x1.shape
    d_ff = w_down.shape[0]
    nf = d_ff // tf
    assert d_ff % tf == 0 and seq % tm == 0 and CONV_WIDTH - 1 <= CONV_HALO <= tm
    kern = functools.partial(_ffn_kernel, tm=tm, tiles_per_seq=seq // tm)
    return pl.pallas_call(
        kern,
        grid=(tokens // tm, nf),
        in_specs=[
            pl.BlockSpec((tm, d_model), lambda i, f: (i, 0)),
            pl.BlockSpec((1, d_model), lambda i, f: (0, 0)),
            pl.BlockSpec((d_model, tf), lambda i, f: (0, f)),
            pl.BlockSpec((d_model, tf), lambda i, f: (0, nf + f)),
            pl.BlockSpec((CONV_WIDTH, tf), lambda i, f: (0, f)),
            pl.BlockSpec((CONV_WIDTH, tf), lambda i, f: (0, nf + f)),
            pl.BlockSpec((1, tf), lambda i, f: (0, f)),
            pl.BlockSpec((1, tf), lambda i, f: (0, nf + f)),
            pl.BlockSpec((tf, d_model), lambda i, f: (f, 0)),
        ],
        out_specs=pl.BlockSpec((tm, d_model), lambda i, f: (i, 0)),
        out_shape=jax.ShapeDtypeStruct((tokens, d_model), F32),
        scratch_shapes=[pltpu.VMEM((tm, d_model), BF16),
                        pltpu.VMEM((2, nf, CONV_HALO, tf), F32)],
        compiler_params=pltpu.CompilerParams(
            dimension_semantics=("arbitrary", "arbitrary"), vmem_limit_bytes=VMEM_LIMIT),
        name="conv_ffn",
    )(x1, ffn_g, w_up, w_up, conv_w, conv_w, conv_b, conv_b, w_down)


def kernel(x, attn_norm_g, w_in, q_norm_g, k_norm_g, rel_bias, pool_w, pool_scale, w_out,
           ffn_norm_g, w_up, conv_w, conv_b, w_down):
    batch, seq, d_model = x.shape
    depth = w_in.shape[0]
    heads = rel_bias.shape[0]
    x2 = x.reshape(batch * seq, d_model)
    bias_tab = _bias_tables(rel_bias)
    for l in range(depth):
        qkvm = _in_proj(x2, attn_norm_g[l][None], w_in[l].astype(BF16), q_norm_g[l][None],
                        k_norm_g[l][None], pool_w[l].astype(BF16), pool_scale[l][None],
                        seq=seq, tm=512)
        a = _moba_attention(qkvm, bias_tab, batch=batch, seq=seq, heads=heads)
        x1 = _out_proj(x2, a, qkvm, w_out[l].astype(BF16), tm=512)
        x2 = _conv_ffn(x1, ffn_norm_g[l][None], w_up[l].astype(BF16), conv_w[l],
                       conv_b[l][None], w_down[l].astype(BF16), seq=seq, tm=512, tf=512)
    return x2.reshape(batch, seq, d_model)
```

```python
import functools
import math

import jax
import jax.numpy as jnp
from jax import lax
from jax.experimental import pallas as pl
from jax.experimental.pallas import tpu as pltpu

HEAD_DIM = 128
MOBA_BLOCK = 256
MOBA_TOPK = 3
POOL_WINDOWS = (2, 4, 8, 16)
REL_BUCKETS = 32
REL_MAX_DIST = 1024
CONV_WIDTH = 3
EPS = 1e-6
NEG = -1e30

POOL_HALO = 16
CONV_HALO = 8
NEAR_DIST = 5

VMEM_LIMIT = 56 * 1024 * 1024

BF16 = jnp.bfloat16
F32 = jnp.float32


def _rel_bucket_thresholds():
    max_exact = REL_BUCKETS // 2
    span = REL_BUCKETS - max_exact
    out = []
    for k in range(1, span):
        edge = max_exact * (REL_MAX_DIST / max_exact) ** (k / span)
        r = round(edge)
        out.append(r if abs(edge - r) < 1e-9 else math.ceil(edge))
    return tuple(out)


def _bias_kernel(rb_ref, o_ref):
    h = pl.program_id(0)
    d = pl.program_id(1)
    shape = (MOBA_BLOCK, MOBA_BLOCK)
    j = lax.broadcasted_iota(jnp.int32, shape, 0)
    i = lax.broadcasted_iota(jnp.int32, shape, 1)
    n = jnp.maximum(d * MOBA_BLOCK + i - j, 0)
    max_exact = REL_BUCKETS // 2
    large = jnp.full(shape, max_exact, jnp.int32)
    for t in _rel_bucket_thresholds():
        large = large + (n >= t).astype(jnp.int32)
    bucket = jnp.where(n < max_exact, n, large)
    val = jnp.zeros(shape, F32)
    for b in range(REL_BUCKETS):
        val = jnp.where(bucket == b, rb_ref[h, b], val)
    o_ref[0, 0] = val


def _bias_tables(rel_bias):
    assert (NEAR_DIST - 1) * MOBA_BLOCK + 1 >= _rel_bucket_thresholds()[-1]
    heads = rel_bias.shape[0]
    return pl.pallas_call(
        _bias_kernel,
        grid=(heads, NEAR_DIST),
        in_specs=[pl.BlockSpec(memory_space=pltpu.SMEM)],
        out_specs=pl.BlockSpec((1, 1, MOBA_BLOCK, MOBA_BLOCK), lambda h, d: (h, d, 0, 0)),
        out_shape=jax.ShapeDtypeStruct((heads, NEAR_DIST, MOBA_BLOCK, MOBA_BLOCK), F32),
        name="bias_tables",
    )(rel_bias)


def _rms(y, g):
    ms = jnp.mean(y * y, axis=-1, keepdims=True)
    return y * lax.rsqrt(ms + EPS) * g


def _inproj_kernel(x_ref, g_ref, w_ref, qg_ref, kg_ref, pw_ref, ps_ref, o_ref, vt_ref, h_sc, halo_sc,
                   *, tm, tiles_per_seq, heads, group_width):
    i = pl.program_id(0)
    j = pl.program_id(1)

    @pl.when(j == 0)
    def _():
        h_sc[...] = _rms(x_ref[...], g_ref[...]).astype(BF16)

    y = jnp.dot(h_sc[...], w_ref[...], preferred_element_type=F32)

    def head_norm(g):
        for h in range(heads):
            sl = slice(h * HEAD_DIM, (h + 1) * HEAD_DIM)
            o_ref[:, sl] = _rms(y[:, sl], g).astype(BF16)

    @pl.when(j == 0)
    def _():
        head_norm(qg_ref[...] * (HEAD_DIM ** -0.5))

    @pl.when(j == 1)
    def _():
        head_norm(kg_ref[...])

    @pl.when(j == 2)
    def _():
        for r in range(tm // MOBA_BLOCK):
            vt_ref[r] = y[r * MOBA_BLOCK:(r + 1) * MOBA_BLOCK, :].T.astype(BF16)

    @pl.when(j == 3)
    def _():
        seq_tile = i % tiles_per_seq
        halo = jnp.where(seq_tile == 0, 0.0, halo_sc[...])
        halo_sc[...] = y[tm - POOL_HALO:, :]
        t = seq_tile * tm + lax.broadcasted_iota(jnp.int32, (tm, 1), 0)
        for g, w in enumerate(POOL_WINDOWS):
            sl = slice(g * group_width, (g + 1) * group_width)
            pg = y[:, sl]
            a = jnp.concatenate([halo[:, sl], pg], axis=0)
            shift = 1
            while shift < w:
                a = a + pltpu.roll(a, shift, axis=0)
                shift *= 2
            count = jnp.minimum(t + 1, w).astype(F32)
            mixed = a[POOL_HALO:, :] / count - pg
            mg = jnp.dot(mixed.astype(BF16), pw_ref[g], preferred_element_type=F32)
            o_ref[:, sl] = (mg * ps_ref[:, sl]).astype(BF16)


def _in_proj(x2, attn_g, w_in, q_g, k_g, pool_w, pool_scale, *, seq, tm):
    tokens, d_model = x2.shape
    in_width = w_in.shape[1]
    tn = in_width // 4
    heads = tn // HEAD_DIM
    groups, group_width, _ = pool_w.shape
    assert groups == len(POOL_WINDOWS) and groups * group_width == tn
    assert seq % tm == 0 and tm >= POOL_HALO and tm % MOBA_BLOCK == 0
    kern = functools.partial(_inproj_kernel, tm=tm, tiles_per_seq=seq // tm, heads=heads,
                             group_width=group_width)
    out_col = lambda i, j: (i, jnp.minimum(j, 1) + j // 3)
    return pl.pallas_call(
        kern,
        grid=(tokens // tm, 4),
        in_specs=[
            pl.BlockSpec((tm, d_model), lambda i, j: (i, 0)),
            pl.BlockSpec((1, d_model), lambda i, j: (0, 0)),
            pl.BlockSpec((d_model, tn), lambda i, j: (0, j)),
            pl.BlockSpec((1, HEAD_DIM), lambda i, j: (0, 0)),
            pl.BlockSpec((1, HEAD_DIM), lambda i, j: (0, 0)),
            pl.BlockSpec((groups, group_width, group_width), lambda i, j: (0, 0, 0)),
            pl.BlockSpec((1, tn), lambda i, j: (0, 0)),
        ],
        out_specs=[pl.BlockSpec((tm, tn), out_col),
                   pl.BlockSpec((tm // MOBA_BLOCK, tn, MOBA_BLOCK), lambda i, j: (i, 0, 0))],
        out_shape=[jax.ShapeDtypeStruct((tokens, 3 * tn), BF16),
                   jax.ShapeDtypeStruct((tokens // MOBA_BLOCK, tn, MOBA_BLOCK), BF16)],
        scratch_shapes=[pltpu.VMEM((tm, d_model), BF16), pltpu.VMEM((POOL_HALO, tn), F32)],
        compiler_params=pltpu.CompilerParams(
            dimension_semantics=("arbitrary", "arbitrary"), vmem_limit_bytes=VMEM_LIMIT),
        name="in_proj",
    )(x2, attn_g, w_in, q_g, k_g, pool_w, pool_scale)


def _nt_dot(a, b):
    return lax.dot_general(a, b, (((1,), (1,)), ((), ())), preferred_element_type=F32)


def _attn_kernel(rb_ref, q_ref, k_ref, vt_ref, bias_ref, o_ref, kmean_sc, sel_sc, m_sc, l_sc, acc_sc,
                 *, nb, heads):
    qb = pl.program_id(1)
    blk = MOBA_BLOCK

    @pl.when(qb == 0)
    def _():
        for n in range(nb):
            kn = k_ref[n * blk:(n + 1) * blk, :].astype(F32)
            kmean_sc[n:n + 1, :] = jnp.mean(kn, axis=0, keepdims=True)

    n_iota = lax.broadcasted_iota(jnp.int32, (nb, blk), 0)
    past = n_iota < qb
    key_i = lax.broadcasted_iota(jnp.int32, (blk, blk), 0)
    qry_i = lax.broadcasted_iota(jnp.int32, (blk, blk), 1)
    causal = key_i <= qry_i
    start = pl.multiple_of(qb * blk, blk)

    head_slices = [slice(h * HEAD_DIM, (h + 1) * HEAD_DIM) for h in range(heads)]

    prods = []
    for hs in head_slices:
        km = kmean_sc[:, hs]
        km_hi = km.astype(BF16)
        km_lo = (km - km_hi.astype(F32)).astype(BF16)
        lhs = jnp.concatenate([km_hi, km_lo, k_ref[pl.ds(start, blk), hs]], axis=0)
        prods.append(_nt_dot(lhs, q_ref[:, hs]))
    probs = []
    for h, hs in enumerate(head_slices):
        g = jnp.where(past, prods[h][:nb] + prods[h][nb:2 * nb], NEG)
        rank = jnp.zeros((nb, blk), jnp.int32)
        for m in range(nb):
            gm = g[m:m + 1, :]
            beats = (gm > g) | ((gm == g) & (n_iota > m))
            rank = rank + beats.astype(jnp.int32)
        sel_sc[h] = ((rank < MOBA_TOPK) & past).astype(F32)
        s = jnp.where(causal, prods[h][2 * nb:] + bias_ref[h, 0], NEG)
        m0 = jnp.max(s, axis=0, keepdims=True)
        p = jnp.exp(s - m0)
        m_sc[h] = m0
        l_sc[h] = jnp.sum(p, axis=0, keepdims=True)
        probs.append(p.astype(BF16))
    for h, hs in enumerate(head_slices):
        acc_sc[h] = jnp.dot(vt_ref[qb, hs, :], probs[h], preferred_element_type=F32)

    def past_block(n, near):
        st = pl.multiple_of(n * blk, blk)
        scores = [_nt_dot(k_ref[pl.ds(st, blk), hs], q_ref[:, hs]) for hs in head_slices]
        scaled = []
        for h, s in enumerate(scores):
            if near:
                s = s + bias_ref[h, qb - n]
                c = 0.0
            else:
                c = rb_ref[h, REL_BUCKETS - 1]
            chosen = sel_sc[h, pl.ds(n, 1), :] > 0.5
            m_tile = jnp.where(chosen, jnp.max(s, axis=0, keepdims=True) + c, NEG)
            m_run = m_sc[h]
            m_new = jnp.maximum(m_run, m_tile)
            alpha = jnp.exp(m_run - m_new)
            p = jnp.exp(s - jnp.where(chosen, m_new - c, -NEG))
            m_sc[h] = m_new
            l_sc[h] = alpha * l_sc[h] + jnp.sum(p, axis=0, keepdims=True)
            scaled.append((alpha, p.astype(BF16)))
        for h, hs in enumerate(head_slices):
            alpha, p = scaled[h]
            acc_sc[h] = alpha * acc_sc[h] + jnp.dot(vt_ref[n, hs, :], p, preferred_element_type=F32)

    def far_body(n, carry):
        past_block(n, near=False)
        return carry

    def near_body(n, carry):
        past_block(n, near=True)
        return carry

    far_end = jnp.maximum(qb - (NEAR_DIST - 1), 0)
    lax.fori_loop(0, far_end, far_body, 0)
    lax.fori_loop(far_end, qb, near_body, 0)

    for h in range(heads):
        hs = slice(h * HEAD_DIM, (h + 1) * HEAD_DIM)
        o_ref[:, hs] = (acc_sc[h] / l_sc[h]).T.astype(BF16)


def _moba_attention(rel_bias, qkm, vt, bias_tab, *, batch, seq, heads):
    tokens = qkm.shape[0]
    nb = seq // MOBA_BLOCK
    width = heads * HEAD_DIM
    assert seq % MOBA_BLOCK == 0
    kern = functools.partial(_attn_kernel, nb=nb, heads=heads)
    once = pl.Buffered(1)
    return pl.pallas_call(
        kern,
        grid=(batch, nb),
        in_specs=[
            pl.BlockSpec(memory_space=pltpu.SMEM),
            pl.BlockSpec((MOBA_BLOCK, width), lambda b, qb: (b * nb + qb, 0)),
            pl.BlockSpec((seq, width), lambda b, qb: (b, 1), pipeline_mode=once),
            pl.BlockSpec((nb, width, MOBA_BLOCK), lambda b, qb: (b, 0, 0), pipeline_mode=once),
            pl.BlockSpec((heads, NEAR_DIST, MOBA_BLOCK, MOBA_BLOCK), lambda b, qb: (0, 0, 0, 0),
                         pipeline_mode=once),
        ],
        out_specs=pl.BlockSpec((MOBA_BLOCK, width), lambda b, qb: (b * nb + qb, 0)),
        out_shape=jax.ShapeDtypeStruct((tokens, width), BF16),
        scratch_shapes=[pltpu.VMEM((nb, width), F32),
                        pltpu.VMEM((heads, nb, MOBA_BLOCK), F32),
                        pltpu.VMEM((heads, 1, MOBA_BLOCK), F32),
                        pltpu.VMEM((heads, 1, MOBA_BLOCK), F32),
                        pltpu.VMEM((heads, HEAD_DIM, MOBA_BLOCK), F32)],
        compiler_params=pltpu.CompilerParams(
            dimension_semantics=("arbitrary", "arbitrary"), vmem_limit_bytes=VMEM_LIMIT),
        name="moba_attn",
    )(rel_bias, qkm, qkm, vt, bias_tab)


def _outproj_kernel(x_ref, a_ref, m_ref, wa_ref, wm_ref, o_ref):
    o_ref[...] = (x_ref[...]
                  + jnp.dot(a_ref[...], wa_ref[...], preferred_element_type=F32)
                  + jnp.dot(m_ref[...], wm_ref[...], preferred_element_type=F32))


def _out_proj(x2, a, qkm, w_out, *, tm):
    tokens, d_model = x2.shape
    half = a.shape[1]
    m_col = qkm.shape[1] // half - 1
    return pl.pallas_call(
        _outproj_kernel,
        grid=(tokens // tm,),
        in_specs=[
            pl.BlockSpec((tm, d_model), lambda i: (i, 0)),
            pl.BlockSpec((tm, half), lambda i: (i, 0)),
            pl.BlockSpec((tm, half), lambda i: (i, m_col)),
            pl.BlockSpec((half, d_model), lambda i: (0, 0)),
            pl.BlockSpec((half, d_model), lambda i: (1, 0)),
        ],
        out_specs=pl.BlockSpec((tm, d_model), lambda i: (i, 0)),
        out_shape=jax.ShapeDtypeStruct((tokens, d_model), F32),
        compiler_params=pltpu.CompilerParams(
            dimension_semantics=("arbitrary",), vmem_limit_bytes=VMEM_LIMIT),
        name="out_proj",
    )(x2, a, qkm, w_out, w_out)


def _ffn_kernel(x_ref, g_ref, wug_ref, wuv_ref, cwg_ref, cwv_ref, cbg_ref, cbv_ref, wd_ref,
                o_ref, h_sc, carry_sc, *, tm, tiles_per_seq):
    i = pl.program_id(0)
    f = pl.program_id(1)

    @pl.when(f == 0)
    def _():
        x = x_ref[...]
        h_sc[...] = _rms(x, g_ref[...]).astype(BF16)
        o_ref[...] = x

    first = (i % tiles_per_seq) == 0
    h = h_sc[...]

    def conv(w_ref, cw_ref, cb_ref, slot):
        u = jnp.dot(h, w_ref[...], preferred_element_type=F32)
        prev = jnp.where(first, 0.0, carry_sc[slot, f])
        carry_sc[slot, f] = u[tm - CONV_HALO:, :]
        e = jnp.concatenate([prev, u], axis=0)
        y = cb_ref[...]
        for tap in range(CONV_WIDTH - 1):
            y = y + cw_ref[tap:tap + 1, :] * pltpu.roll(e, CONV_WIDTH - 1 - tap, axis=0)[CONV_HALO:, :]
        return y + cw_ref[CONV_WIDTH - 1:CONV_WIDTH, :] * u

    yg = conv(wug_ref, cwg_ref, cbg_ref, 0)
    yv = conv(wuv_ref, cwv_ref, cbv_ref, 1)
    act = yg / (1.0 + jnp.exp(-yg)) * yv
    o_ref[...] += jnp.dot(act.astype(BF16), wd_ref[...], preferred_element_type=F32)


def _conv_ffn(x1, ffn_g, w_up, conv_w, conv_b, w_down, *, seq, tm, tf):
    tokens, d_model = x1.shape
    d_ff = w_down.shape[0]
    nf = d_ff // tf
    assert d_ff % tf == 0 and seq % tm == 0 and CONV_WIDTH - 1 <= CONV_HALO <= tm
    kern = functools.partial(_ffn_kernel, tm=tm, tiles_per_seq=seq // tm)
    return pl.pallas_call(
        kern,
        grid=(tokens // tm, nf),
        in_specs=[
            pl.BlockSpec((tm, d_model), lambda i, f: (i, 0)),
            pl.BlockSpec((1, d_model), lambda i, f: (0, 0)),
            pl.BlockSpec((d_model, tf), lambda i, f: (0, f)),
            pl.BlockSpec((d_model, tf), lambda i, f: (0, nf + f)),
            pl.BlockSpec((CONV_WIDTH, tf), lambda i, f: (0, f)),
            pl.BlockSpec((CONV_WIDTH, tf), lambda i, f: (0, nf + f)),
            pl.BlockSpec((1, tf), lambda i, f: (0, f)),
            pl.BlockSpec((1, tf), lambda i, f: (0, nf + f)),
            pl.BlockSpec((tf, d_model), lambda i, f: (f, 0)),
        ],
        out_specs=pl.BlockSpec((tm, d_model), lambda i, f: (i, 0)),
        out_shape=jax.ShapeDtypeStruct((tokens, d_model), F32),
        scratch_shapes=[pltpu.VMEM((tm, d_model), BF16),
                        pltpu.VMEM((2, nf, CONV_HALO, tf), F32)],
        compiler_params=pltpu.CompilerParams(
            dimension_semantics=("arbitrary", "arbitrary"), vmem_limit_bytes=VMEM_LIMIT),
        name="conv_ffn",
    )(x1, ffn_g, w_up, w_up, conv_w, conv_w, conv_b, conv_b, w_down)


def kernel(x, attn_norm_g, w_in, q_norm_g, k_norm_g, rel_bias, pool_w, pool_scale, w_out,
           ffn_norm_g, w_up, conv_w, conv_b, w_down):
    batch, seq, d_model = x.shape
    depth = w_in.shape[0]
    heads = rel_bias.shape[0]
    x2 = x.reshape(batch * seq, d_model)
    bias_tab = _bias_tables(rel_bias)
    for l in range(depth):
        qkm, vt = _in_proj(x2, attn_norm_g[l][None], w_in[l].astype(BF16), q_norm_g[l][None],
                           k_norm_g[l][None], pool_w[l].astype(BF16), pool_scale[l][None],
                           seq=seq, tm=512)
        a = _moba_attention(rel_bias, qkm, vt, bias_tab, batch=batch, seq=seq, heads=heads)
        x1 = _out_proj(x2, a, qkm, w_out[l].astype(BF16), tm=512)
        x2 = _conv_ffn(x1, ffn_norm_g[l][None], w_up[l].astype(BF16), conv_w[l],
                       conv_b[l][None], w_down[l].astype(BF16), seq=seq, tm=512, tf=512)
    return x2.reshape(batch, seq, d_model)
```

```python
import functools
import math

import jax
import jax.numpy as jnp
from jax import lax
from jax.experimental import pallas as pl
from jax.experimental.pallas import tpu as pltpu

HEAD_DIM = 128
MOBA_BLOCK = 256
MOBA_TOPK = 3
POOL_WINDOWS = (2, 4, 8, 16)
REL_BUCKETS = 32
REL_MAX_DIST = 1024
CONV_WIDTH = 3
EPS = 1e-6
NEG = -1e30

POOL_HALO = 16
CONV_HALO = 8
NEAR_DIST = 5

ONES_ROWS = 16
LOG2E = math.log2(math.e)

VMEM_LIMIT = 56 * 1024 * 1024

BF16 = jnp.bfloat16
F32 = jnp.float32


def _rel_bucket_thresholds():
    max_exact = REL_BUCKETS // 2
    span = REL_BUCKETS - max_exact
    out = []
    for k in range(1, span):
        edge = max_exact * (REL_MAX_DIST / max_exact) ** (k / span)
        r = round(edge)
        out.append(r if abs(edge - r) < 1e-9 else math.ceil(edge))
    return tuple(out)


def _bias_kernel(rb_ref, o_ref):
    h = pl.program_id(0)
    d = pl.program_id(1)
    blk = MOBA_BLOCK
    shape = (8, 2 * blk)
    m = lax.broadcasted_iota(jnp.int32, shape, 1)
    n = jnp.maximum(d * blk + jnp.where(m < blk, m, m - 2 * blk), 0)
    max_exact = REL_BUCKETS // 2
    large = jnp.full(shape, max_exact, jnp.int32)
    for t in _rel_bucket_thresholds():
        large = large + (n >= t).astype(jnp.int32)
    bucket = jnp.where(n < max_exact, n, large)
    val = jnp.zeros(shape, F32)
    for b in range(REL_BUCKETS):
        val = jnp.where(bucket == b, rb_ref[h, b], val)
    strip = jnp.broadcast_to(val[0:1, :], (blk, 2 * blk))
    tile = pltpu.roll(strip, 0, axis=1, stride=1, stride_axis=0)[:, :blk]
    o_ref[0, 0] = tile * LOG2E


def _bias_tables(rel_bias):
    assert (NEAR_DIST - 1) * MOBA_BLOCK + 1 >= _rel_bucket_thresholds()[-1]
    heads = rel_bias.shape[0]
    return pl.pallas_call(
        _bias_kernel,
        grid=(heads, NEAR_DIST),
        in_specs=[pl.BlockSpec(memory_space=pltpu.SMEM)],
        out_specs=pl.BlockSpec((1, 1, MOBA_BLOCK, MOBA_BLOCK), lambda h, d: (h, d, 0, 0)),
        out_shape=jax.ShapeDtypeStruct((heads, NEAR_DIST, MOBA_BLOCK, MOBA_BLOCK), F32),
        name="bias_tables",
    )(rel_bias)


def _rms(y, g):
    ms = jnp.mean(y * y, axis=-1, keepdims=True)
    return y * lax.rsqrt(ms + EPS) * g


def _inproj_kernel(x_ref, g_ref, w_ref, qg_ref, kg_ref, pw_ref, ps_ref, o_ref, vt_ref, h_sc, halo_sc,
                   *, tm, tiles_per_seq, heads, group_width):
    i = pl.program_id(0)
    j = pl.program_id(1)

    @pl.when(j == 0)
    def _():
        h_sc[...] = _rms(x_ref[...], g_ref[...]).astype(BF16)

    y = jnp.dot(h_sc[...], w_ref[...], preferred_element_type=F32)

    def head_norm(g):
        for h in range(heads):
            sl = slice(h * HEAD_DIM, (h + 1) * HEAD_DIM)
            o_ref[:, sl] = _rms(y[:, sl], g).astype(BF16)

    @pl.when(j == 0)
    def _():
        head_norm(qg_ref[...] * (HEAD_DIM ** -0.5 * LOG2E))

    @pl.when(j == 1)
    def _():
        head_norm(kg_ref[...])

    @pl.when(j == 2)
    def _():
        for r in range(tm // MOBA_BLOCK):
            vt_ref[r] = y[r * MOBA_BLOCK:(r + 1) * MOBA_BLOCK, :].T.astype(BF16)

    @pl.when(j == 3)
    def _():
        seq_tile = i % tiles_per_seq
        halo = jnp.where(seq_tile == 0, 0.0, halo_sc[...])
        halo_sc[...] = y[tm - POOL_HALO:, :]
        t = seq_tile * tm + lax.broadcasted_iota(jnp.int32, (tm, 1), 0)
        for g, w in enumerate(POOL_WINDOWS):
            sl = slice(g * group_width, (g + 1) * group_width)
            pg = y[:, sl]
            a = jnp.concatenate([halo[:, sl], pg], axis=0)
            shift = 1
            while shift < w:
                a = a + pltpu.roll(a, shift, axis=0)
                shift *= 2
            count = jnp.minimum(t + 1, w).astype(F32)
            mixed = a[POOL_HALO:, :] / count - pg
            mg = jnp.dot(mixed.astype(BF16), pw_ref[g], preferred_element_type=F32)
            o_ref[:, sl] = (mg * ps_ref[:, sl]).astype(BF16)


def _in_proj(x2, attn_g, w_in, q_g, k_g, pool_w, pool_scale, *, seq, tm):
    tokens, d_model = x2.shape
    in_width = w_in.shape[1]
    tn = in_width // 4
    heads = tn // HEAD_DIM
    groups, group_width, _ = pool_w.shape
    assert groups == len(POOL_WINDOWS) and groups * group_width == tn
    assert seq % tm == 0 and tm >= POOL_HALO and tm % MOBA_BLOCK == 0
    kern = functools.partial(_inproj_kernel, tm=tm, tiles_per_seq=seq // tm, heads=heads,
                             group_width=group_width)
    out_col = lambda i, j: (i, jnp.minimum(j, 1) + j // 3)
    return pl.pallas_call(
        kern,
        grid=(tokens // tm, 4),
        in_specs=[
            pl.BlockSpec((tm, d_model), lambda i, j: (i, 0)),
            pl.BlockSpec((1, d_model), lambda i, j: (0, 0)),
            pl.BlockSpec((d_model, tn), lambda i, j: (0, j)),
            pl.BlockSpec((1, HEAD_DIM), lambda i, j: (0, 0)),
            pl.BlockSpec((1, HEAD_DIM), lambda i, j: (0, 0)),
            pl.BlockSpec((groups, group_width, group_width), lambda i, j: (0, 0, 0)),
            pl.BlockSpec((1, tn), lambda i, j: (0, 0)),
        ],
        out_specs=[pl.BlockSpec((tm, tn), out_col),
                   pl.BlockSpec((tm // MOBA_BLOCK, tn, MOBA_BLOCK), lambda i, j: (i, 0, 0))],
        out_shape=[jax.ShapeDtypeStruct((tokens, 3 * tn), BF16),
                   jax.ShapeDtypeStruct((tokens // MOBA_BLOCK, tn, MOBA_BLOCK), BF16)],
        scratch_shapes=[pltpu.VMEM((tm, d_model), BF16), pltpu.VMEM((POOL_HALO, tn), F32)],
        compiler_params=pltpu.CompilerParams(
            dimension_semantics=("arbitrary", "arbitrary"), vmem_limit_bytes=VMEM_LIMIT),
        name="in_proj",
    )(x2, attn_g, w_in, q_g, k_g, pool_w, pool_scale)


def _nt_dot(a, b):
    return lax.dot_general(a, b, (((1,), (1,)), ((), ())), preferred_element_type=F32)


def _attn_kernel(rb_ref, q_ref, k_ref, vt_ref, bias_ref, o_ref, kmean_sc, sel_sc, m_sc, acc_sc,
                 *, nb, heads):
    qb = pl.program_id(1)
    blk = MOBA_BLOCK

    @pl.when(qb == 0)
    def _():
        for n in range(nb):
            kn = k_ref[n * blk:(n + 1) * blk, :].astype(F32)
            kmean_sc[n:n + 1, :] = jnp.mean(kn, axis=0, keepdims=True)

    n_iota = lax.broadcasted_iota(jnp.int32, (nb, blk), 0)
    past = n_iota < qb
    key_i = lax.broadcasted_iota(jnp.int32, (blk, blk), 0)
    qry_i = lax.broadcasted_iota(jnp.int32, (blk, blk), 1)
    causal = key_i <= qry_i
    start = pl.multiple_of(qb * blk, blk)
    head_slices = [slice(h * HEAD_DIM, (h + 1) * HEAD_DIM) for h in range(heads)]
    ones = jnp.ones((ONES_ROWS, blk), BF16)

    def values(n, hs):
        return jnp.concatenate([vt_ref[n, hs, :], ones], axis=0)

    def scores(n):
        st = pl.multiple_of(n * blk, blk)
        return tuple(_nt_dot(k_ref[pl.ds(st, blk), hs], q_ref[:, hs]) for hs in head_slices)


    prods = []
    for hs in head_slices:
        km = kmean_sc[:, hs]
        km_hi = km.astype(BF16)
        km_lo = (km - km_hi.astype(F32)).astype(BF16)
        lhs = jnp.concatenate([km_hi, km_lo, k_ref[pl.ds(start, blk), hs]], axis=0)
        prods.append(_nt_dot(lhs, q_ref[:, hs]))
    probs = []
    for h, hs in enumerate(head_slices):
        g = jnp.where(past, prods[h][:nb] + prods[h][nb:2 * nb], NEG)
        rank = jnp.zeros((nb, blk), jnp.int32)
        for m in range(nb):
            gm = g[m:m + 1, :]
            beats = (gm > g) | ((gm == g) & (n_iota > m))
            rank = rank + beats.astype(jnp.int32)
        sel_sc[h] = ((rank < MOBA_TOPK) & past).astype(F32)
        s = jnp.where(causal, prods[h][2 * nb:] + bias_ref[h, 0], NEG)
        m0 = jnp.max(s, axis=0, keepdims=True)
        m_sc[h] = m0
        probs.append(jnp.exp2(s - m0).astype(BF16))
    for h, hs in enumerate(head_slices):
        acc_sc[h] = jnp.dot(values(qb, hs), probs[h], preferred_element_type=F32)

    def past_block(n, near):
        scaled = []
        for h, s in enumerate(scores(n)):
            if near:
                s = s + bias_ref[h, qb - n]
                c = 0.0
            else:
                c = rb_ref[h, REL_BUCKETS - 1] * LOG2E
            chosen = sel_sc[h, pl.ds(n, 1), :] > 0.5
            m_tile = jnp.where(chosen, jnp.max(s, axis=0, keepdims=True) + c, NEG)
            m_run = m_sc[h]
            m_new = jnp.maximum(m_run, m_tile)
            m_sc[h] = m_new
            p = jnp.exp2(s - jnp.where(chosen, m_new - c, -NEG))
            scaled.append((jnp.exp2(m_run - m_new), p.astype(BF16)))
        for h, hs in enumerate(head_slices):
            alpha, p = scaled[h]
            acc_sc[h] = alpha * acc_sc[h] + jnp.dot(values(n, hs), p, preferred_element_type=F32)

    def far_body(n, carry):
        past_block(n, near=False)
        return carry

    def near_body(n, carry):
        past_block(n, near=True)
        return carry

    far_end = jnp.maximum(qb - (NEAR_DIST - 1), 0)
    lax.fori_loop(0, far_end, far_body, 0)
    lax.fori_loop(far_end, qb, near_body, 0)

    for h, hs in enumerate(head_slices):
        acc = acc_sc[h]
        o_ref[:, hs] = (acc[:HEAD_DIM] / acc[HEAD_DIM:HEAD_DIM + 1]).T.astype(BF16)


def _moba_attention(rel_bias, qkm, vt, bias_tab, *, batch, seq, heads):
    tokens = qkm.shape[0]
    nb = seq // MOBA_BLOCK
    width = heads * HEAD_DIM
    assert seq % MOBA_BLOCK == 0
    kern = functools.partial(_attn_kernel, nb=nb, heads=heads)
    once = pl.Buffered(1)
    return pl.pallas_call(
        kern,
        grid=(batch, nb),
        in_specs=[
            pl.BlockSpec(memory_space=pltpu.SMEM),
            pl.BlockSpec((MOBA_BLOCK, width), lambda b, qb: (b * nb + qb, 0)),
            pl.BlockSpec((seq, width), lambda b, qb: (b, 1), pipeline_mode=once),
            pl.BlockSpec((nb, width, MOBA_BLOCK), lambda b, qb: (b, 0, 0), pipeline_mode=once),
            pl.BlockSpec((heads, NEAR_DIST, MOBA_BLOCK, MOBA_BLOCK), lambda b, qb: (0, 0, 0, 0),
                         pipeline_mode=once),
        ],
        out_specs=pl.BlockSpec((MOBA_BLOCK, width), lambda b, qb: (b * nb + qb, 0)),
        out_shape=jax.ShapeDtypeStruct((tokens, width), BF16),
        scratch_shapes=[pltpu.VMEM((nb, width), F32),
                        pltpu.VMEM((heads, nb, MOBA_BLOCK), F32),
                        pltpu.VMEM((heads, 1, MOBA_BLOCK), F32),
                        pltpu.VMEM((heads, HEAD_DIM + ONES_ROWS, MOBA_BLOCK), F32)],
        compiler_params=pltpu.CompilerParams(
            dimension_semantics=("arbitrary", "arbitrary"), vmem_limit_bytes=VMEM_LIMIT),
        name="moba_attn",
    )(rel_bias, qkm, qkm, vt, bias_tab)


def _outproj_kernel(x_ref, a_ref, m_ref, wa_ref, wm_ref, o_ref):
    o_ref[...] = (x_ref[...]
                  + jnp.dot(a_ref[...], wa_ref[...], preferred_element_type=F32)
                  + jnp.dot(m_ref[...], wm_ref[...], preferred_element_type=F32))


def _out_proj(x2, a, qkm, w_out, *, tm):
    tokens, d_model = x2.shape
    half = a.shape[1]
    m_col = qkm.shape[1] // half - 1
    return pl.pallas_call(
        _outproj_kernel,
        grid=(tokens // tm,),
        in_specs=[
            pl.BlockSpec((tm, d_model), lambda i: (i, 0)),
            pl.BlockSpec((tm, half), lambda i: (i, 0)),
            pl.BlockSpec((tm, half), lambda i: (i, m_col)),
            pl.BlockSpec((half, d_model), lambda i: (0, 0)),
            pl.BlockSpec((half, d_model), lambda i: (1, 0)),
        ],
        out_specs=pl.BlockSpec((tm, d_model), lambda i: (i, 0)),
        out_shape=jax.ShapeDtypeStruct((tokens, d_model), F32),
        compiler_params=pltpu.CompilerParams(
            dimension_semantics=("arbitrary",), vmem_limit_bytes=VMEM_LIMIT),
        name="out_proj",
    )(x2, a, qkm, w_out, w_out)


def _ffn_kernel(x_ref, g_ref, wug_ref, wuv_ref, cwg_ref, cwv_ref, cbg_ref, cbv_ref, wd_ref,
                o_ref, h_sc, carry_sc, *, tm, tiles_per_seq):
    i = pl.program_id(0)
    f = pl.program_id(1)

    @pl.when(f == 0)
    def _():
        x = x_ref[...]
        h_sc[...] = _rms(x, g_ref[...]).astype(BF16)
        o_ref[...] = x

    first = (i % tiles_per_seq) == 0
    h = h_sc[...]

    def conv(w_ref, cw_ref, cb_ref, slot):
        u = jnp.dot(h, w_ref[...], preferred_element_type=F32)
        prev = jnp.where(first, 0.0, carry_sc[slot, f])
        carry_sc[slot, f] = u[tm - CONV_HALO:, :]
        e = jnp.concatenate([prev, u], axis=0)
        y = cb_ref[...]
        for tap in range(CONV_WIDTH - 1):
            y = y + cw_ref[tap:tap + 1, :] * pltpu.roll(e, CONV_WIDTH - 1 - tap, axis=0)[CONV_HALO:, :]
        return y + cw_ref[CONV_WIDTH - 1:CONV_WIDTH, :] * u

    yg = conv(wug_ref, cwg_ref, cbg_ref, 0)
    yv = conv(wuv_ref, cwv_ref, cbv_ref, 1)
    act = yg / (1.0 + jnp.exp(-yg)) * yv
    o_ref[...] += jnp.dot(act.astype(BF16), wd_ref[...], preferred_element_type=F32)


def _conv_ffn(x1, ffn_g, w_up, conv_w, conv_b, w_down, *, seq, tm, tf):
    tokens, d_model = x1.shape
    d_ff = w_down.shape[0]
    nf = d_ff // tf
    assert d_ff % tf == 0 and seq % tm == 0 and CONV_WIDTH - 1 <= CONV_HALO <= tm
    kern = functools.partial(_ffn_kernel, tm=tm, tiles_per_seq=seq // tm)
    return pl.pallas_call(
        kern,
        grid=(tokens // tm, nf),
        in_specs=[
            pl.BlockSpec((tm, d_model), lambda i, f: (i, 0)),
            pl.BlockSpec((1, d_model), lambda i, f: (0, 0)),
            pl.BlockSpec((d_model, tf), lambda i, f: (0, f)),
            pl.BlockSpec((d_model, tf), lambda i, f: (0, nf + f)),
            pl.BlockSpec((CONV_WIDTH, tf), lambda i, f: (0, f)),
            pl.BlockSpec((CONV_WIDTH, tf), lambda i, f: (0, nf + f)),
            pl.BlockSpec((1, tf), lambda i, f: (0, f)),
            pl.BlockSpec((1, tf), lambda i, f: (0, nf + f)),
            pl.BlockSpec((tf, d_model), lambda i, f: (f, 0)),
        ],
        out_specs=pl.BlockSpec((tm, d_model), lambda i, f: (i, 0)),
        out_shape=jax.ShapeDtypeStruct((tokens, d_model), F32),
        scratch_shapes=[pltpu.VMEM((tm, d_model), BF16),
                        pltpu.VMEM((2, nf, CONV_HALO, tf), F32)],
        compiler_params=pltpu.CompilerParams(
            dimension_semantics=("arbitrary", "arbitrary"), vmem_limit_bytes=VMEM_LIMIT),
        name="conv_ffn",
    )(x1, ffn_g, w_up, w_up, conv_w, conv_w, conv_b, conv_b, w_down)


def kernel(x, attn_norm_g, w_in, q_norm_g, k_norm_g, rel_bias, pool_w, pool_scale, w_out,
           ffn_norm_g, w_up, conv_w, conv_b, w_down):
    batch, seq, d_model = x.shape
    depth = w_in.shape[0]
    heads = rel_bias.shape[0]
    x2 = x.reshape(batch * seq, d_model)
    bias_tab = _bias_tables(rel_bias)
    for l in range(depth):
        qkm, vt = _in_proj(x2, attn_norm_g[l][None], w_in[l].astype(BF16), q_norm_g[l][None],
                           k_norm_g[l][None], pool_w[l].astype(BF16), pool_scale[l][None],
                           seq=seq, tm=1024)
        a = _moba_attention(rel_bias, qkm, vt, bias_tab, batch=batch, seq=seq, heads=heads)
        x1 = _out_proj(x2, a, qkm, w_out[l].astype(BF16), tm=512)
        x2 = _conv_ffn(x1, ffn_norm_g[l][None], w_up[l].astype(BF16), conv_w[l],
                       conv_b[l][None], w_down[l].astype(BF16), seq=seq, tm=512, tf=512)
    return x2.reshape(batch, seq, d_model)
```

```python
import functools
import math

import jax
import jax.numpy as jnp
from jax import lax
from jax.experimental import pallas as pl
from jax.experimental.pallas import tpu as pltpu

HEAD_DIM = 128
MOBA_BLOCK = 256
MOBA_TOPK = 3
POOL_WINDOWS = (2, 4, 8, 16)
REL_BUCKETS = 32
REL_MAX_DIST = 1024
CONV_WIDTH = 3
EPS = 1e-6
NEG = -1e30

POOL_HALO = 16
CONV_HALO = 8
NEAR_DIST = 5

ONES_ROWS = 16
LOG2E = math.log2(math.e)
FAR_GROUP = 2

VMEM_LIMIT = 56 * 1024 * 1024

BF16 = jnp.bfloat16
F32 = jnp.float32


def _rel_bucket_thresholds():
    max_exact = REL_BUCKETS // 2
    span = REL_BUCKETS - max_exact
    out = []
    for k in range(1, span):
        edge = max_exact * (REL_MAX_DIST / max_exact) ** (k / span)
        r = round(edge)
        out.append(r if abs(edge - r) < 1e-9 else math.ceil(edge))
    return tuple(out)


def _bias_kernel(rb_ref, o_ref):
    h = pl.program_id(0)
    blk = MOBA_BLOCK
    shape = (8, 2 * blk)
    m = lax.broadcasted_iota(jnp.int32, shape, 1)
    offset = jnp.where(m < blk, m, m - 2 * blk)
    max_exact = REL_BUCKETS // 2
    for d in range(NEAR_DIST):
        n = jnp.maximum(d * blk + offset, 0)
        large = jnp.full(shape, max_exact, jnp.int32)
        for t in _rel_bucket_thresholds():
            large = large + (n >= t).astype(jnp.int32)
        bucket = jnp.where(n < max_exact, n, large)
        val = jnp.zeros(shape, F32)
        for b in range(REL_BUCKETS):
            val = jnp.where(bucket == b, rb_ref[h, b], val)
        val = val * LOG2E
        if d == 0:
            val = jnp.where(offset < 0, NEG, val)
        strip = jnp.broadcast_to(val[0:1, :], (blk, 2 * blk))
        o_ref[0, d] = pltpu.roll(strip, 0, axis=1, stride=1, stride_axis=0)[:, :blk]


def _bias_tables(rel_bias):
    assert (NEAR_DIST - 1) * MOBA_BLOCK + 1 >= _rel_bucket_thresholds()[-1]
    heads = rel_bias.shape[0]
    return pl.pallas_call(
        _bias_kernel,
        grid=(heads,),
        in_specs=[pl.BlockSpec(memory_space=pltpu.SMEM)],
        out_specs=pl.BlockSpec((1, NEAR_DIST, MOBA_BLOCK, MOBA_BLOCK), lambda h: (h, 0, 0, 0)),
        out_shape=jax.ShapeDtypeStruct((heads, NEAR_DIST, MOBA_BLOCK, MOBA_BLOCK), F32),
        name="bias_tables",
    )(rel_bias)


def _rms(y, g):
    ms = jnp.mean(y * y, axis=-1, keepdims=True)
    return y * lax.rsqrt(ms + EPS) * g


def _inproj_kernel(x_ref, g_ref, w_ref, qg_ref, kg_ref, pw_ref, ps_ref, o_ref, vt_ref, h_sc, halo_sc,
                   *, tm, tiles_per_seq, heads, group_width):
    i = pl.program_id(0)
    j = pl.program_id(1)

    @pl.when(j == 0)
    def _():
        h_sc[...] = _rms(x_ref[...], g_ref[...]).astype(BF16)

    y = jnp.dot(h_sc[...], w_ref[...], preferred_element_type=F32)

    def head_norm(g):
        for h in range(heads):
            sl = slice(h * HEAD_DIM, (h + 1) * HEAD_DIM)
            o_ref[:, sl] = _rms(y[:, sl], g).astype(BF16)

    @pl.when(j == 0)
    def _():
        head_norm(qg_ref[...] * (HEAD_DIM ** -0.5 * LOG2E))

    @pl.when(j == 1)
    def _():
        head_norm(kg_ref[...])

    @pl.when(j == 2)
    def _():
        for r in range(tm // MOBA_BLOCK):
            vt_ref[r] = y[r * MOBA_BLOCK:(r + 1) * MOBA_BLOCK, :].T.astype(BF16)

    @pl.when(j == 3)
    def _():
        seq_tile = i % tiles_per_seq
        halo = jnp.where(seq_tile == 0, 0.0, halo_sc[...])
        halo_sc[...] = y[tm - POOL_HALO:, :]
        t = seq_tile * tm + lax.broadcasted_iota(jnp.int32, (tm, 1), 0)
        for g, w in enumerate(POOL_WINDOWS):
            sl = slice(g * group_width, (g + 1) * group_width)
            pg = y[:, sl]
            a = jnp.concatenate([halo[:, sl], pg], axis=0)
            shift = 1
            while shift < w:
                a = a + pltpu.roll(a, shift, axis=0)
                shift *= 2
            count = jnp.minimum(t + 1, w).astype(F32)
            mixed = a[POOL_HALO:, :] / count - pg
            mg = jnp.dot(mixed.astype(BF16), pw_ref[g], preferred_element_type=F32)
            o_ref[:, sl] = (mg * ps_ref[:, sl]).astype(BF16)


def _in_proj(x2, attn_g, w_in, q_g, k_g, pool_w, pool_scale, *, seq, tm):
    tokens, d_model = x2.shape
    in_width = w_in.shape[1]
    tn = in_width // 4
    heads = tn // HEAD_DIM
    groups, group_width, _ = pool_w.shape
    assert groups == len(POOL_WINDOWS) and groups * group_width == tn
    assert seq % tm == 0 and tm >= POOL_HALO and tm % MOBA_BLOCK == 0
    kern = functools.partial(_inproj_kernel, tm=tm, tiles_per_seq=seq // tm, heads=heads,
                             group_width=group_width)
    out_col = lambda i, j: (i, jnp.minimum(j, 1) + j // 3)
    return pl.pallas_call(
        kern,
        grid=(tokens // tm, 4),
        in_specs=[
            pl.BlockSpec((tm, d_model), lambda i, j: (i, 0)),
            pl.BlockSpec((1, d_model), lambda i, j: (0, 0)),
            pl.BlockSpec((d_model, tn), lambda i, j: (0, j)),
            pl.BlockSpec((1, HEAD_DIM), lambda i, j: (0, 0)),
            pl.BlockSpec((1, HEAD_DIM), lambda i, j: (0, 0)),
            pl.BlockSpec((groups, group_width, group_width), lambda i, j: (0, 0, 0)),
            pl.BlockSpec((1, tn), lambda i, j: (0, 0)),
        ],
        out_specs=[pl.BlockSpec((tm, tn), out_col),
                   pl.BlockSpec((tm // MOBA_BLOCK, tn, MOBA_BLOCK), lambda i, j: (i, 0, 0))],
        out_shape=[jax.ShapeDtypeStruct((tokens, 3 * tn), BF16),
                   jax.ShapeDtypeStruct((tokens // MOBA_BLOCK, tn, MOBA_BLOCK), BF16)],
        scratch_shapes=[pltpu.VMEM((tm, d_model), BF16), pltpu.VMEM((POOL_HALO, tn), F32)],
        compiler_params=pltpu.CompilerParams(
            dimension_semantics=("arbitrary", "arbitrary"), vmem_limit_bytes=VMEM_LIMIT),
        name="in_proj",
    )(x2, attn_g, w_in, q_g, k_g, pool_w, pool_scale)


def _nt_dot(a, b):
    return lax.dot_general(a, b, (((1,), (1,)), ((), ())), preferred_element_type=F32)


def _attn_kernel(rb_ref, q_ref, k_ref, vt_ref, bias_ref, o_ref, kmean_sc, sel_sc, m_sc, acc_sc,
                 *, nb, heads):
    qb = pl.program_id(1)
    blk = MOBA_BLOCK

    @pl.when(qb == 0)
    def _():
        for n in range(nb):
            kn = k_ref[n * blk:(n + 1) * blk, :].astype(F32)
            kmean_sc[n:n + 1, :] = jnp.mean(kn, axis=0, keepdims=True)

    n_iota = lax.broadcasted_iota(jnp.int32, (nb, blk), 0)
    past = n_iota < qb
    start = pl.multiple_of(qb * blk, blk)
    head_slices = [slice(h * HEAD_DIM, (h + 1) * HEAD_DIM) for h in range(heads)]
    ones = jnp.ones((ONES_ROWS, blk), BF16)

    def values(n, hs):
        return jnp.concatenate([vt_ref[n, hs, :], ones], axis=0)


    prods = []
    for hs in head_slices:
        km = kmean_sc[:, hs]
        km_hi = km.astype(BF16)
        km_lo = (km - km_hi.astype(F32)).astype(BF16)
        lhs = jnp.concatenate([km_hi, km_lo, k_ref[pl.ds(start, blk), hs]], axis=0)
        prods.append(_nt_dot(lhs, q_ref[:, hs]))
    probs = []
    for h, hs in enumerate(head_slices):
        g = jnp.where(past, prods[h][:nb] + prods[h][nb:2 * nb], NEG)
        rank = jnp.zeros((nb, blk), jnp.int32)
        for m in range(nb):
            gm = g[m:m + 1, :]
            beats = (gm > g) | ((gm == g) & (n_iota > m))
            rank = rank + beats.astype(jnp.int32)
        sel_sc[h] = ((rank < MOBA_TOPK) & past).astype(F32)
        s = prods[h][2 * nb:] + bias_ref[h, 0]
        m0 = jnp.max(s, axis=0, keepdims=True)
        m_sc[h] = m0
        probs.append(jnp.exp2(s - m0).astype(BF16))
    for h, hs in enumerate(head_slices):
        acc_sc[h] = jnp.dot(values(qb, hs), probs[h], preferred_element_type=F32)

    def past_blocks(n0, count, dists):
        st = pl.multiple_of(n0 * blk, blk)
        prods = [_nt_dot(k_ref[pl.ds(st, count * blk), hs], q_ref[:, hs]) for hs in head_slices]
        scaled = []
        for h, s_all in enumerate(prods):
            far_bias = rb_ref[h, REL_BUCKETS - 1] * LOG2E
            tiles, m_tile = [], None
            for j, dist in enumerate(dists):
                s = s_all[j * blk:(j + 1) * blk]
                if dist is None:
                    c = far_bias
                else:
                    s = s + bias_ref[h, dist]
                    c = 0.0
                chosen = sel_sc[h, pl.ds(n0 + j, 1), :] > 0.5
                m_j = jnp.where(chosen, jnp.max(s, axis=0, keepdims=True) + c, NEG)
                m_tile = m_j if m_tile is None else jnp.maximum(m_tile, m_j)
                tiles.append((s, chosen, c))
            m_run = m_sc[h]
            m_new = jnp.maximum(m_run, m_tile)
            m_sc[h] = m_new
            p = [jnp.exp2(s - jnp.where(chosen, m_new - c, -NEG)).astype(BF16) for s, chosen, c in tiles]
            scaled.append((jnp.exp2(m_run - m_new), jnp.concatenate(p, axis=0)))
        for h, hs in enumerate(head_slices):
            alpha, p = scaled[h]
            vals = jnp.concatenate([values(n0 + j, hs) for j in range(count)], axis=1)
            acc_sc[h] = alpha * acc_sc[h] + jnp.dot(vals, p, preferred_element_type=F32)

    def far_group(i, carry):
        past_blocks(i * FAR_GROUP, FAR_GROUP, (None,) * FAR_GROUP)
        return carry

    def far_single(n, carry):
        past_blocks(n, 1, (None,))
        return carry

    def near_single(n, carry):
        past_blocks(n, 1, (qb - n,))
        return carry

    near_count = NEAR_DIST - 1
    far_end = jnp.maximum(qb - near_count, 0)
    far_groups = far_end // FAR_GROUP
    lax.fori_loop(0, far_groups, far_group, 0)
    lax.fori_loop(far_groups * FAR_GROUP, far_end, far_single, 0)
    lax.fori_loop(0, jnp.where(qb < near_count, qb, 0), near_single, 0)

    @pl.when(qb >= near_count)
    def _():
        past_blocks(qb - near_count, near_count, tuple(range(near_count, 0, -1)))

    for h, hs in enumerate(head_slices):
        acc = acc_sc[h]
        o_ref[:, hs] = (acc[:HEAD_DIM] / acc[HEAD_DIM:HEAD_DIM + 1]).T.astype(BF16)


def _moba_attention(rel_bias, qkm, vt, bias_tab, *, batch, seq, heads):
    tokens = qkm.shape[0]
    nb = seq // MOBA_BLOCK
    width = heads * HEAD_DIM
    assert seq % MOBA_BLOCK == 0
    kern = functools.partial(_attn_kernel, nb=nb, heads=heads)
    once = pl.Buffered(1)
    return pl.pallas_call(
        kern,
        grid=(batch, nb),
        in_specs=[
            pl.BlockSpec(memory_space=pltpu.SMEM),
            pl.BlockSpec((MOBA_BLOCK, width), lambda b, qb: (b * nb + qb, 0)),
            pl.BlockSpec((seq, width), lambda b, qb: (b, 1), pipeline_mode=once),
            pl.BlockSpec((nb, width, MOBA_BLOCK), lambda b, qb: (b, 0, 0), pipeline_mode=once),
            pl.BlockSpec((heads, NEAR_DIST, MOBA_BLOCK, MOBA_BLOCK), lambda b, qb: (0, 0, 0, 0),
                         pipeline_mode=once),
        ],
        out_specs=pl.BlockSpec((MOBA_BLOCK, width), lambda b, qb: (b * nb + qb, 0)),
        out_shape=jax.ShapeDtypeStruct((tokens, width), BF16),
        scratch_shapes=[pltpu.VMEM((nb, width), F32),
                        pltpu.VMEM((heads, nb, MOBA_BLOCK), F32),
                        pltpu.VMEM((heads, 1, MOBA_BLOCK), F32),
                        pltpu.VMEM((heads, HEAD_DIM + ONES_ROWS, MOBA_BLOCK), F32)],
        compiler_params=pltpu.CompilerParams(
            dimension_semantics=("arbitrary", "arbitrary"), vmem_limit_bytes=VMEM_LIMIT),
        name="moba_attn",
    )(rel_bias, qkm, qkm, vt, bias_tab)


def _outproj_kernel(x_ref, a_ref, m_ref, wa_ref, wm_ref, o_ref):
    o_ref[...] = (x_ref[...]
                  + jnp.dot(a_ref[...], wa_ref[...], preferred_element_type=F32)
                  + jnp.dot(m_ref[...], wm_ref[...], preferred_element_type=F32))


def _out_proj(x2, a, qkm, w_out, *, tm):
    tokens, d_model = x2.shape
    half = a.shape[1]
    m_col = qkm.shape[1] // half - 1
    return pl.pallas_call(
        _outproj_kernel,
        grid=(tokens // tm,),
        in_specs=[
            pl.BlockSpec((tm, d_model), lambda i: (i, 0)),
            pl.BlockSpec((tm, half), lambda i: (i, 0)),
            pl.BlockSpec((tm, half), lambda i: (i, m_col)),
            pl.BlockSpec((half, d_model), lambda i: (0, 0)),
            pl.BlockSpec((half, d_model), lambda i: (1, 0)),
        ],
        out_specs=pl.BlockSpec((tm, d_model), lambda i: (i, 0)),
        out_shape=jax.ShapeDtypeStruct((tokens, d_model), F32),
        compiler_params=pltpu.CompilerParams(
            dimension_semantics=("arbitrary",), vmem_limit_bytes=VMEM_LIMIT),
        name="out_proj",
    )(x2, a, qkm, w_out, w_out)


def _ffn_kernel(x_ref, g_ref, wug_ref, wuv_ref, cwg_ref, cwv_ref, cbg_ref, cbv_ref, wd_ref,
                o_ref, h_sc, carry_sc, *, tm, tiles_per_seq):
    i = pl.program_id(0)
    f = pl.program_id(1)

    @pl.when(f == 0)
    def _():
        x = x_ref[...]
        h_sc[...] = _rms(x, g_ref[...]).astype(BF16)
        o_ref[...] = x

    first = (i % tiles_per_seq) == 0
    h = h_sc[...]

    def conv(w_ref, cw_ref, cb_ref, slot):
        u = jnp.dot(h, w_ref[...], preferred_element_type=F32)
        prev = jnp.where(first, 0.0, carry_sc[slot, f])
        carry_sc[slot, f] = u[tm - CONV_HALO:, :]
        e = jnp.concatenate([prev, u], axis=0)
        y = cb_ref[...]
        for tap in range(CONV_WIDTH - 1):
            y = y + cw_ref[tap:tap + 1, :] * pltpu.roll(e, CONV_WIDTH - 1 - tap, axis=0)[CONV_HALO:, :]
        return y + cw_ref[CONV_WIDTH - 1:CONV_WIDTH, :] * u

    yg = conv(wug_ref, cwg_ref, cbg_ref, 0)
    yv = conv(wuv_ref, cwv_ref, cbv_ref, 1)
    act = yg / (1.0 + jnp.exp2(yg * -LOG2E)) * yv
    o_ref[...] += jnp.dot(act.astype(BF16), wd_ref[...], preferred_element_type=F32)


def _conv_ffn(x1, ffn_g, w_up, conv_w, conv_b, w_down, *, seq, tm, tf):
    tokens, d_model = x1.shape
    d_ff = w_down.shape[0]
    nf = d_ff // tf
    assert d_ff % tf == 0 and seq % tm == 0 and CONV_WIDTH - 1 <= CONV_HALO <= tm
    kern = functools.partial(_ffn_kernel, tm=tm, tiles_per_seq=seq // tm)
    return pl.pallas_call(
        kern,
        grid=(tokens // tm, nf),
        in_specs=[
            pl.BlockSpec((tm, d_model), lambda i, f: (i, 0)),
            pl.BlockSpec((1, d_model), lambda i, f: (0, 0)),
            pl.BlockSpec((d_model, tf), lambda i, f: (0, f)),
            pl.BlockSpec((d_model, tf), lambda i, f: (0, nf + f)),
            pl.BlockSpec((CONV_WIDTH, tf), lambda i, f: (0, f)),
            pl.BlockSpec((CONV_WIDTH, tf), lambda i, f: (0, nf + f)),
            pl.BlockSpec((1, tf), lambda i, f: (0, f)),
            pl.BlockSpec((1, tf), lambda i, f: (0, nf + f)),
            pl.BlockSpec((tf, d_model), lambda i, f: (f, 0)),
        ],
        out_specs=pl.BlockSpec((tm, d_model), lambda i, f: (i, 0)),
        out_shape=jax.ShapeDtypeStruct((tokens, d_model), F32),
        scratch_shapes=[pltpu.VMEM((tm, d_model), BF16),
                        pltpu.VMEM((2, nf, CONV_HALO, tf), F32)],
        compiler_params=pltpu.CompilerParams(
            dimension_semantics=("arbitrary", "arbitrary"), vmem_limit_bytes=VMEM_LIMIT),
        name="conv_ffn",
    )(x1, ffn_g, w_up, w_up, conv_w, conv_w, conv_b, conv_b, w_down)


def kernel(x, attn_norm_g, w_in, q_norm_g, k_norm_g, rel_bias, pool_w, pool_scale, w_out,
           ffn_norm_g, w_up, conv_w, conv_b, w_down):
    batch, seq, d_model = x.shape
    depth = w_in.shape[0]
    heads = rel_bias.shape[0]
    x2 = x.reshape(batch * seq, d_model)
    bias_tab = _bias_tables(rel_bias)
    for l in range(depth):
        qkm, vt = _in_proj(x2, attn_norm_g[l][None], w_in[l].astype(BF16), q_norm_g[l][None],
                           k_norm_g[l][None], pool_w[l].astype(BF16), pool_scale[l][None],
                           seq=seq, tm=1024)
        a = _moba_attention(rel_bias, qkm, vt, bias_tab, batch=batch, seq=seq, heads=heads)
        x1 = _out_proj(x2, a, qkm, w_out[l].astype(BF16), tm=512)
        x2 = _conv_ffn(x1, ffn_norm_g[l][None], w_up[l].astype(BF16), conv_w[l],
                       conv_b[l][None], w_down[l].astype(BF16), seq=seq, tm=512, tf=512)
    return x2.reshape(batch, seq, d_model)
```

```python
import functools
import math

import jax
import jax.numpy as jnp
from jax import lax
from jax.experimental import pallas as pl
from jax.experimental.pallas import tpu as pltpu

HEAD_DIM = 128
MOBA_BLOCK = 256
MOBA_TOPK = 3
POOL_WINDOWS = (2, 4, 8, 16)
REL_BUCKETS = 32
REL_MAX_DIST = 1024
CONV_WIDTH = 3
EPS = 1e-6
NEG = -1e30

POOL_HALO = 16
CONV_HALO = 8
NEAR_DIST = 5

ONES_ROWS = 16
LOG2E = math.log2(math.e)
FAR_GROUP = 2

VMEM_LIMIT = 56 * 1024 * 1024

BF16 = jnp.bfloat16
F32 = jnp.float32
LANES = 128
BF16_SUBLANES = 16


def _rel_bucket_thresholds():
    max_exact = REL_BUCKETS // 2
    span = REL_BUCKETS - max_exact
    out = []
    for k in range(1, span):
        edge = max_exact * (REL_MAX_DIST / max_exact) ** (k / span)
        r = round(edge)
        out.append(r if abs(edge - r) < 1e-9 else math.ceil(edge))
    return tuple(out)


def _bias_kernel(rb_ref, o_ref):
    h = pl.program_id(0)
    blk = MOBA_BLOCK
    shape = (8, 2 * blk)
    m = lax.broadcasted_iota(jnp.int32, shape, 1)
    offset = jnp.where(m < blk, m, m - 2 * blk)
    max_exact = REL_BUCKETS // 2
    for d in range(NEAR_DIST):
        n = jnp.maximum(d * blk + offset, 0)
        large = jnp.full(shape, max_exact, jnp.int32)
        for t in _rel_bucket_thresholds():
            large = large + (n >= t).astype(jnp.int32)
        bucket = jnp.where(n < max_exact, n, large)
        val = jnp.zeros(shape, F32)
        for b in range(REL_BUCKETS):
            val = jnp.where(bucket == b, rb_ref[h, b], val)
        val = val * LOG2E
        if d == 0:
            val = jnp.where(offset < 0, NEG, val)
        strip = jnp.broadcast_to(val[0:1, :], (blk, 2 * blk))
        o_ref[0, d] = pltpu.roll(strip, 0, axis=1, stride=1, stride_axis=0)[:, :blk]


def _bias_tables(rel_bias):
    assert (NEAR_DIST - 1) * MOBA_BLOCK + 1 >= _rel_bucket_thresholds()[-1]
    heads = rel_bias.shape[0]
    return pl.pallas_call(
        _bias_kernel,
        grid=(heads,),
        in_specs=[pl.BlockSpec(memory_space=pltpu.SMEM)],
        out_specs=pl.BlockSpec((1, NEAR_DIST, MOBA_BLOCK, MOBA_BLOCK), lambda h: (h, 0, 0, 0)),
        out_shape=jax.ShapeDtypeStruct((heads, NEAR_DIST, MOBA_BLOCK, MOBA_BLOCK), F32),
        name="bias_tables",
    )(rel_bias)


def _rms(y, g):
    ms = jnp.mean(y * y, axis=-1, keepdims=True)
    return y * lax.rsqrt(ms + EPS) * g


def _inproj_kernel(x_ref, g_ref, w_ref, qg_ref, kg_ref, pw_ref, ps_ref, *rest,
                   tm, tiles_per_seq, heads, group_width, n_side):
    side_in, (o_ref, vt_ref), rest = rest[:n_side], rest[n_side:n_side + 2], rest[n_side + 2:]
    side_out, (h_sc, halo_sc) = rest[:n_side], rest[n_side:]
    i = pl.program_id(0)
    j = pl.program_id(1)

    for src, dst in zip(side_in, side_out):
        dst[...] = src[...].astype(BF16)

    @pl.when(j == 0)
    def _():
        h_sc[...] = _rms(x_ref[...], g_ref[...]).astype(BF16)

    y = jnp.dot(h_sc[...], w_ref[...], preferred_element_type=F32)

    def head_norm(g):
        for h in range(heads):
            sl = slice(h * HEAD_DIM, (h + 1) * HEAD_DIM)
            o_ref[:, sl] = _rms(y[:, sl], g).astype(BF16)

    @pl.when(j == 0)
    def _():
        head_norm(qg_ref[...] * (HEAD_DIM ** -0.5 * LOG2E))

    @pl.when(j == 1)
    def _():
        head_norm(kg_ref[...])

    @pl.when(j == 2)
    def _():
        for r in range(tm // MOBA_BLOCK):
            vt_ref[r] = y[r * MOBA_BLOCK:(r + 1) * MOBA_BLOCK, :].T.astype(BF16)

    @pl.when(j == 3)
    def _():
        seq_tile = i % tiles_per_seq
        halo = jnp.where(seq_tile == 0, 0.0, halo_sc[...])
        halo_sc[...] = y[tm - POOL_HALO:, :]
        t = seq_tile * tm + lax.broadcasted_iota(jnp.int32, (tm, 1), 0)
        for g, w in enumerate(POOL_WINDOWS):
            sl = slice(g * group_width, (g + 1) * group_width)
            pg = y[:, sl]
            a = jnp.concatenate([halo[:, sl], pg], axis=0)
            shift = 1
            while shift < w:
                a = a + pltpu.roll(a, shift, axis=0)
                shift *= 2
            count = jnp.minimum(t + 1, w).astype(F32)
            mixed = a[POOL_HALO:, :] / count - pg
            mg = jnp.dot(mixed.astype(BF16), pw_ref[g], preferred_element_type=F32)
            o_ref[:, sl] = (mg * ps_ref[:, sl]).astype(BF16)


def _in_proj(x2, attn_g, w_in, q_g, k_g, pool_w, pool_scale, side, *, seq, tm):
    tokens, d_model = x2.shape
    in_width = w_in.shape[1]
    tn = in_width // 4
    heads = tn // HEAD_DIM
    groups, group_width, _ = pool_w.shape
    assert groups == len(POOL_WINDOWS) and groups * group_width == tn
    assert seq % tm == 0 and tm >= POOL_HALO and tm % MOBA_BLOCK == 0
    kern = functools.partial(_inproj_kernel, tm=tm, tiles_per_seq=seq // tm, heads=heads,
                             group_width=group_width, n_side=len(side))
    out_col = lambda i, j: (i, jnp.minimum(j, 1) + j // 3)
    steps = (tokens // tm) * 4
    slab_rows = steps * BF16_SUBLANES
    assert all(a.size % (slab_rows * LANES) == 0 for a in side)
    slabs = [a.reshape(slab_rows, a.size // slab_rows) for a in side]
    slab_specs = [pl.BlockSpec((BF16_SUBLANES, a.shape[1]), lambda i, j: (i * 4 + j, 0)) for a in slabs]
    outs = pl.pallas_call(
        kern,
        grid=(tokens // tm, 4),
        in_specs=[
            pl.BlockSpec((tm, d_model), lambda i, j: (i, 0)),
            pl.BlockSpec((1, d_model), lambda i, j: (0, 0)),
            pl.BlockSpec((d_model, tn), lambda i, j: (0, j)),
            pl.BlockSpec((1, HEAD_DIM), lambda i, j: (0, 0)),
            pl.BlockSpec((1, HEAD_DIM), lambda i, j: (0, 0)),
            pl.BlockSpec((groups, group_width, group_width), lambda i, j: (0, 0, 0)),
            pl.BlockSpec((1, tn), lambda i, j: (0, 0)),
        ] + slab_specs,
        out_specs=[pl.BlockSpec((tm, tn), out_col),
                   pl.BlockSpec((tm // MOBA_BLOCK, tn, MOBA_BLOCK), lambda i, j: (i, 0, 0))] + slab_specs,
        out_shape=[jax.ShapeDtypeStruct((tokens, 3 * tn), BF16),
                   jax.ShapeDtypeStruct((tokens // MOBA_BLOCK, tn, MOBA_BLOCK), BF16)]
        + [jax.ShapeDtypeStruct(a.shape, BF16) for a in slabs],
        scratch_shapes=[pltpu.VMEM((tm, d_model), BF16), pltpu.VMEM((POOL_HALO, tn), F32)],
        compiler_params=pltpu.CompilerParams(
            dimension_semantics=("arbitrary", "arbitrary"), vmem_limit_bytes=VMEM_LIMIT),
        name="in_proj",
    )(x2, attn_g, w_in, q_g, k_g, pool_w, pool_scale, *slabs)
    return outs[0], outs[1], [o.reshape(a.shape) for o, a in zip(outs[2:], side)]


def _nt_dot(a, b):
    return lax.dot_general(a, b, (((1,), (1,)), ((), ())), preferred_element_type=F32)


def _attn_kernel(rb_ref, q_ref, k_ref, vt_ref, bias_ref, o_ref, kmean_sc, sel_sc, m_sc, acc_sc,
                 *, nb, heads):
    qb = pl.program_id(1)
    blk = MOBA_BLOCK

    @pl.when(qb == 0)
    def _():
        for n in range(nb):
            kn = k_ref[n * blk:(n + 1) * blk, :].astype(F32)
            kmean_sc[n:n + 1, :] = jnp.mean(kn, axis=0, keepdims=True)

    n_iota = lax.broadcasted_iota(jnp.int32, (nb, blk), 0)
    past = n_iota < qb
    start = pl.multiple_of(qb * blk, blk)
    head_slices = [slice(h * HEAD_DIM, (h + 1) * HEAD_DIM) for h in range(heads)]
    ones = jnp.ones((ONES_ROWS, blk), BF16)

    def values(n, hs):
        return jnp.concatenate([vt_ref[n, hs, :], ones], axis=0)


    prods = []
    for hs in head_slices:
        km = kmean_sc[:, hs]
        km_hi = km.astype(BF16)
        km_lo = (km - km_hi.astype(F32)).astype(BF16)
        lhs = jnp.concatenate([km_hi, km_lo, k_ref[pl.ds(start, blk), hs]], axis=0)
        prods.append(_nt_dot(lhs, q_ref[:, hs]))
    probs = []
    for h, hs in enumerate(head_slices):
        g = jnp.where(past, prods[h][:nb] + prods[h][nb:2 * nb], NEG)
        rank = jnp.zeros((nb, blk), jnp.int32)
        for m in range(nb):
            gm = g[m:m + 1, :]
            beats = (gm > g) | ((gm == g) & (n_iota > m))
            rank = rank + beats.astype(jnp.int32)
        sel_sc[h] = ((rank < MOBA_TOPK) & past).astype(F32)
        s = prods[h][2 * nb:] + bias_ref[h, 0]
        m0 = jnp.max(s, axis=0, keepdims=True)
        m_sc[h] = m0
        probs.append(jnp.exp2(s - m0).astype(BF16))
    for h, hs in enumerate(head_slices):
        acc_sc[h] = jnp.dot(values(qb, hs), probs[h], preferred_element_type=F32)

    def past_blocks(n0, count, dists):
        st = pl.multiple_of(n0 * blk, blk)
        prods = [_nt_dot(k_ref[pl.ds(st, count * blk), hs], q_ref[:, hs]) for hs in head_slices]
        scaled = []
        for h, s_all in enumerate(prods):
            far_bias = rb_ref[h, REL_BUCKETS - 1] * LOG2E
            tiles, m_tile = [], None
            for j, dist in enumerate(dists):
                s = s_all[j * blk:(j + 1) * blk]
                if dist is None:
                    c = far_bias
                else:
                    s = s + bias_ref[h, dist]
                    c = 0.0
                chosen = sel_sc[h, pl.ds(n0 + j, 1), :] > 0.5
                m_j = jnp.where(chosen, jnp.max(s, axis=0, keepdims=True) + c, NEG)
                m_tile = m_j if m_tile is None else jnp.maximum(m_tile, m_j)
                tiles.append((s, chosen, c))
            m_run = m_sc[h]
            m_new = jnp.maximum(m_run, m_tile)
            m_sc[h] = m_new
            p = [jnp.exp2(s - jnp.where(chosen, m_new - c, -NEG)).astype(BF16) for s, chosen, c in tiles]
            scaled.append((jnp.exp2(m_run - m_new), jnp.concatenate(p, axis=0)))
        for h, hs in enumerate(head_slices):
            alpha, p = scaled[h]
            vals = jnp.concatenate([values(n0 + j, hs) for j in range(count)], axis=1)
            acc_sc[h] = alpha * acc_sc[h] + jnp.dot(vals, p, preferred_element_type=F32)

    def far_group(i, carry):
        past_blocks(i * FAR_GROUP, FAR_GROUP, (None,) * FAR_GROUP)
        return carry

    def far_single(n, carry):
        past_blocks(n, 1, (None,))
        return carry

    def near_single(n, carry):
        past_blocks(n, 1, (qb - n,))
        return carry

    near_count = NEAR_DIST - 1
    far_end = jnp.maximum(qb - near_count, 0)
    far_groups = far_end // FAR_GROUP
    lax.fori_loop(0, far_groups, far_group, 0)
    lax.fori_loop(far_groups * FAR_GROUP, far_end, far_single, 0)
    lax.fori_loop(0, jnp.where(qb < near_count, qb, 0), near_single, 0)

    @pl.when(qb >= near_count)
    def _():
        past_blocks(qb - near_count, near_count, tuple(range(near_count, 0, -1)))

    for h, hs in enumerate(head_slices):
        acc = acc_sc[h]
        o_ref[:, hs] = (acc[:HEAD_DIM] / acc[HEAD_DIM:HEAD_DIM + 1]).T.astype(BF16)


def _moba_attention(rel_bias, qkm, vt, bias_tab, *, batch, seq, heads):
    tokens = qkm.shape[0]
    nb = seq // MOBA_BLOCK
    width = heads * HEAD_DIM
    assert seq % MOBA_BLOCK == 0
    kern = functools.partial(_attn_kernel, nb=nb, heads=heads)
    return pl.pallas_call(
        kern,
        grid=(batch, nb),
        in_specs=[
            pl.BlockSpec(memory_space=pltpu.SMEM),
            pl.BlockSpec((MOBA_BLOCK, width), lambda b, qb: (b * nb + qb, 0)),
            pl.BlockSpec((seq, width), lambda b, qb: (b, 1)),
            pl.BlockSpec((nb, width, MOBA_BLOCK), lambda b, qb: (b, 0, 0)),
            pl.BlockSpec((heads, NEAR_DIST, MOBA_BLOCK, MOBA_BLOCK), lambda b, qb: (0, 0, 0, 0),
                         pipeline_mode=pl.Buffered(1)),
        ],
        out_specs=pl.BlockSpec((MOBA_BLOCK, width), lambda b, qb: (b * nb + qb, 0)),
        out_shape=jax.ShapeDtypeStruct((tokens, width), BF16),
        scratch_shapes=[pltpu.VMEM((nb, width), F32),
                        pltpu.VMEM((heads, nb, MOBA_BLOCK), F32),
                        pltpu.VMEM((heads, 1, MOBA_BLOCK), F32),
                        pltpu.VMEM((heads, HEAD_DIM + ONES_ROWS, MOBA_BLOCK), F32)],
        compiler_params=pltpu.CompilerParams(
            dimension_semantics=("arbitrary", "arbitrary"), vmem_limit_bytes=VMEM_LIMIT),
        name="moba_attn",
    )(rel_bias, qkm, qkm, vt, bias_tab)


def _outproj_kernel(x_ref, a_ref, m_ref, wa_ref, wm_ref, o_ref):
    o_ref[...] = (x_ref[...]
                  + jnp.dot(a_ref[...], wa_ref[...], preferred_element_type=F32)
                  + jnp.dot(m_ref[...], wm_ref[...], preferred_element_type=F32))


def _out_proj(x2, a, qkm, w_out, *, tm):
    tokens, d_model = x2.shape
    half = a.shape[1]
    m_col = qkm.shape[1] // half - 1
    return pl.pallas_call(
        _outproj_kernel,
        grid=(tokens // tm,),
        in_specs=[
            pl.BlockSpec((tm, d_model), lambda i: (i, 0)),
            pl.BlockSpec((tm, half), lambda i: (i, 0)),
            pl.BlockSpec((tm, half), lambda i: (i, m_col)),
            pl.BlockSpec((half, d_model), lambda i: (0, 0)),
            pl.BlockSpec((half, d_model), lambda i: (1, 0)),
        ],
        out_specs=pl.BlockSpec((tm, d_model), lambda i: (i, 0)),
        out_shape=jax.ShapeDtypeStruct((tokens, d_model), F32),
        compiler_params=pltpu.CompilerParams(
            dimension_semantics=("arbitrary",), vmem_limit_bytes=VMEM_LIMIT),
        name="out_proj",
    )(x2, a, qkm, w_out, w_out)


def _ffn_kernel(x_ref, g_ref, wug_ref, wuv_ref, cwg_ref, cwv_ref, cbg_ref, cbv_ref, wd_ref,
                o_ref, h_sc, carry_sc, *, tm, tiles_per_seq):
    i = pl.program_id(0)
    f = pl.program_id(1)

    @pl.when(f == 0)
    def _():
        x = x_ref[...]
        h_sc[...] = _rms(x, g_ref[...]).astype(BF16)
        o_ref[...] = x

    first = (i % tiles_per_seq) == 0
    h = h_sc[...]

    def conv(w_ref, cw_ref, cb_ref, slot):
        u = jnp.dot(h, w_ref[...], preferred_element_type=F32)
        prev = jnp.where(first, 0.0, carry_sc[slot, f])
        carry_sc[slot, f] = u[tm - CONV_HALO:, :]
        e = jnp.concatenate([prev, u], axis=0)
        y = cb_ref[...]
        for tap in range(CONV_WIDTH - 1):
            y = y + cw_ref[tap:tap + 1, :] * pltpu.roll(e, CONV_WIDTH - 1 - tap, axis=0)[CONV_HALO:, :]
        return y + cw_ref[CONV_WIDTH - 1:CONV_WIDTH, :] * u

    yg = conv(wug_ref, cwg_ref, cbg_ref, 0)
    yv = conv(wuv_ref, cwv_ref, cbv_ref, 1)
    act = yg / (1.0 + jnp.exp2(yg * -LOG2E)) * yv
    o_ref[...] += jnp.dot(act.astype(BF16), wd_ref[...], preferred_element_type=F32)


def _conv_ffn(x1, ffn_g, w_up, conv_w, conv_b, w_down, *, seq, tm, tf):
    tokens, d_model = x1.shape
    d_ff = w_down.shape[0]
    nf = d_ff // tf
    assert d_ff % tf == 0 and seq % tm == 0 and CONV_WIDTH - 1 <= CONV_HALO <= tm
    kern = functools.partial(_ffn_kernel, tm=tm, tiles_per_seq=seq // tm)
    return pl.pallas_call(
        kern,
        grid=(tokens // tm, nf),
        in_specs=[
            pl.BlockSpec((tm, d_model), lambda i, f: (i, 0)),
            pl.BlockSpec((1, d_model), lambda i, f: (0, 0)),
            pl.BlockSpec((d_model, tf), lambda i, f: (0, f)),
            pl.BlockSpec((d_model, tf), lambda i, f: (0, nf + f)),
            pl.BlockSpec((CONV_WIDTH, tf), lambda i, f: (0, f)),
            pl.BlockSpec((CONV_WIDTH, tf), lambda i, f: (0, nf + f)),
            pl.BlockSpec((1, tf), lambda i, f: (0, f)),
            pl.BlockSpec((1, tf), lambda i, f: (0, nf + f)),
            pl.BlockSpec((tf, d_model), lambda i, f: (f, 0)),
        ],
        out_specs=pl.BlockSpec((tm, d_model), lambda i, f: (i, 0)),
        out_shape=jax.ShapeDtypeStruct((tokens, d_model), F32),
        scratch_shapes=[pltpu.VMEM((tm, d_model), BF16),
                        pltpu.VMEM((2, nf, CONV_HALO, tf), F32)],
        compiler_params=pltpu.CompilerParams(
            dimension_semantics=("arbitrary", "arbitrary"), vmem_limit_bytes=VMEM_LIMIT),
        name="conv_ffn",
    )(x1, ffn_g, w_up, w_up, conv_w, conv_w, conv_b, conv_b, w_down)


def kernel(x, attn_norm_g, w_in, q_norm_g, k_norm_g, rel_bias, pool_w, pool_scale, w_out,
           ffn_norm_g, w_up, conv_w, conv_b, w_down):
    batch, seq, d_model = x.shape
    depth = w_in.shape[0]
    heads = rel_bias.shape[0]
    x2 = x.reshape(batch * seq, d_model)
    bias_tab = _bias_tables(rel_bias)
    for l in range(depth):
        qkm, vt, (w_out_l, w_up_l, w_down_l) = _in_proj(
            x2, attn_norm_g[l][None], w_in[l].astype(BF16), q_norm_g[l][None], k_norm_g[l][None],
            pool_w[l].astype(BF16), pool_scale[l][None], (w_out[l], w_up[l], w_down[l]), seq=seq, tm=1024)
        a = _moba_attention(rel_bias, qkm, vt, bias_tab, batch=batch, seq=seq, heads=heads)
        x1 = _out_proj(x2, a, qkm, w_out_l, tm=512)
        x2 = _conv_ffn(x1, ffn_norm_g[l][None], w_up_l, conv_w[l], conv_b[l][None], w_down_l,
                       seq=seq, tm=512, tf=512)
    return x2.reshape(batch, seq, d_model)
```

```python
import functools
import math

import jax
import jax.numpy as jnp
from jax import lax
from jax.experimental import pallas as pl
from jax.experimental.pallas import tpu as pltpu

HEAD_DIM = 128
MOBA_BLOCK = 256
MOBA_TOPK = 3
POOL_WINDOWS = (2, 4, 8, 16)
REL_BUCKETS = 32
REL_MAX_DIST = 1024
CONV_WIDTH = 3
EPS = 1e-6
NEG = -1e30

POOL_HALO = 16
CONV_HALO = 8
NEAR_DIST = 5

ONES_ROWS = 16
LOG2E = math.log2(math.e)
FAR_GROUP = 2

VMEM_LIMIT = 56 * 1024 * 1024

BF16 = jnp.bfloat16
F32 = jnp.float32
BF16_SUBLANES = 16


def _rel_bucket_thresholds():
    max_exact = REL_BUCKETS // 2
    span = REL_BUCKETS - max_exact
    out = []
    for k in range(1, span):
        edge = max_exact * (REL_MAX_DIST / max_exact) ** (k / span)
        r = round(edge)
        out.append(r if abs(edge - r) < 1e-9 else math.ceil(edge))
    return tuple(out)


def _bias_kernel(rb_ref, o_ref):
    h = pl.program_id(0)
    blk = MOBA_BLOCK
    shape = (8, 2 * blk)
    m = lax.broadcasted_iota(jnp.int32, shape, 1)
    offset = jnp.where(m < blk, m, m - 2 * blk)
    max_exact = REL_BUCKETS // 2
    for d in range(NEAR_DIST):
        n = jnp.maximum(d * blk + offset, 0)
        large = jnp.full(shape, max_exact, jnp.int32)
        for t in _rel_bucket_thresholds():
            large = large + (n >= t).astype(jnp.int32)
        bucket = jnp.where(n < max_exact, n, large)
        val = jnp.zeros(shape, F32)
        for b in range(REL_BUCKETS):
            val = jnp.where(bucket == b, rb_ref[h, b], val)
        val = val * LOG2E
        if d == 0:
            val = jnp.where(offset < 0, NEG, val)
        strip = jnp.broadcast_to(val[0:1, :], (blk, 2 * blk))
        o_ref[0, d] = pltpu.roll(strip, 0, axis=1, stride=1, stride_axis=0)[:, :blk]


def _bias_tables(rel_bias):
    assert (NEAR_DIST - 1) * MOBA_BLOCK + 1 >= _rel_bucket_thresholds()[-1]
    heads = rel_bias.shape[0]
    return pl.pallas_call(
        _bias_kernel,
        grid=(heads,),
        in_specs=[pl.BlockSpec(memory_space=pltpu.SMEM)],
        out_specs=pl.BlockSpec((1, NEAR_DIST, MOBA_BLOCK, MOBA_BLOCK), lambda h: (h, 0, 0, 0)),
        out_shape=jax.ShapeDtypeStruct((heads, NEAR_DIST, MOBA_BLOCK, MOBA_BLOCK), F32),
        name="bias_tables",
    )(rel_bias)


def _rms(y, g):
    ms = jnp.mean(y * y, axis=-1, keepdims=True)
    return y * lax.rsqrt(ms + EPS) * g


def _inproj_kernel(x_ref, g_ref, w_ref, qg_ref, kg_ref, pw_ref, ps_ref, *rest,
                   tm, tiles_per_seq, heads, group_width, n_side):
    side_in, (o_ref, vt_ref), rest = rest[:n_side], rest[n_side:n_side + 2], rest[n_side + 2:]
    side_out, (h_sc, halo_sc) = rest[:n_side], rest[n_side:]
    i = pl.program_id(0)
    j = pl.program_id(1)

    for src, dst in zip(side_in, side_out):
        dst[...] = src[...].astype(BF16)

    @pl.when(j == 0)
    def _():
        h_sc[...] = _rms(x_ref[...], g_ref[...]).astype(BF16)

    y = jnp.dot(h_sc[...], w_ref[...], preferred_element_type=F32)

    def head_norm(g):
        for h in range(heads):
            sl = slice(h * HEAD_DIM, (h + 1) * HEAD_DIM)
            o_ref[:, sl] = _rms(y[:, sl], g).astype(BF16)

    @pl.when(j == 0)
    def _():
        head_norm(qg_ref[...] * (HEAD_DIM ** -0.5 * LOG2E))

    @pl.when(j == 1)
    def _():
        head_norm(kg_ref[...])

    @pl.when(j == 2)
    def _():
        for r in range(tm // MOBA_BLOCK):
            vt_ref[r] = y[r * MOBA_BLOCK:(r + 1) * MOBA_BLOCK, :].T.astype(BF16)

    @pl.when(j == 3)
    def _():
        seq_tile = i % tiles_per_seq
        halo = jnp.where(seq_tile == 0, 0.0, halo_sc[...])
        halo_sc[...] = y[tm - POOL_HALO:, :]
        t = seq_tile * tm + lax.broadcasted_iota(jnp.int32, (tm, 1), 0)
        for g, w in enumerate(POOL_WINDOWS):
            sl = slice(g * group_width, (g + 1) * group_width)
            pg = y[:, sl]
            a = jnp.concatenate([halo[:, sl], pg], axis=0)
            shift = 1
            while shift < w:
                a = a + pltpu.roll(a, shift, axis=0)
                shift *= 2
            count = jnp.minimum(t + 1, w).astype(F32)
            mixed = a[POOL_HALO:, :] / count - pg
            mg = jnp.dot(mixed.astype(BF16), pw_ref[g], preferred_element_type=F32)
            o_ref[:, sl] = (mg * ps_ref[:, sl]).astype(BF16)


def _in_proj(x2, attn_g, w_in, q_g, k_g, pool_w, pool_scale, side, layer, *, seq, tm):
    tokens, d_model = x2.shape
    in_width = w_in.shape[1]
    tn = in_width // 4
    heads = tn // HEAD_DIM
    groups, group_width, _ = pool_w.shape
    assert groups == len(POOL_WINDOWS) and groups * group_width == tn
    assert seq % tm == 0 and tm >= POOL_HALO and tm % MOBA_BLOCK == 0
    kern = functools.partial(_inproj_kernel, tm=tm, tiles_per_seq=seq // tm, heads=heads,
                             group_width=group_width, n_side=len(side))
    out_col = lambda i, j: (i, jnp.minimum(j, 1) + j // 3)
    steps = (tokens // tm) * 4
    slab_in, slab_out = [], []
    for a in side:
        rows = a.shape[1]
        hold = next(h for h in (1, 2, 4, 8) if steps % h == 0 and rows % (steps // h * BF16_SUBLANES) == 0)
        slab = (rows // (steps // hold), a.shape[2])
        slab_in.append(pl.BlockSpec((None,) + slab, lambda i, j, hold=hold: (layer, (i * 4 + j) // hold, 0)))
        slab_out.append(pl.BlockSpec(slab, lambda i, j, hold=hold: ((i * 4 + j) // hold, 0)))
    outs = pl.pallas_call(
        kern,
        grid=(tokens // tm, 4),
        in_specs=[
            pl.BlockSpec((tm, d_model), lambda i, j: (i, 0)),
            pl.BlockSpec((1, d_model), lambda i, j: (0, 0)),
            pl.BlockSpec((d_model, tn), lambda i, j: (0, j)),
            pl.BlockSpec((1, HEAD_DIM), lambda i, j: (0, 0)),
            pl.BlockSpec((1, HEAD_DIM), lambda i, j: (0, 0)),
            pl.BlockSpec((groups, group_width, group_width), lambda i, j: (0, 0, 0)),
            pl.BlockSpec((1, tn), lambda i, j: (0, 0)),
        ] + slab_in,
        out_specs=[pl.BlockSpec((tm, tn), out_col),
                   pl.BlockSpec((tm // MOBA_BLOCK, tn, MOBA_BLOCK), lambda i, j: (i, 0, 0))] + slab_out,
        out_shape=[jax.ShapeDtypeStruct((tokens, 3 * tn), BF16),
                   jax.ShapeDtypeStruct((tokens // MOBA_BLOCK, tn, MOBA_BLOCK), BF16)]
        + [jax.ShapeDtypeStruct(a.shape[1:], BF16) for a in side],
        scratch_shapes=[pltpu.VMEM((tm, d_model), BF16), pltpu.VMEM((POOL_HALO, tn), F32)],
        compiler_params=pltpu.CompilerParams(
            dimension_semantics=("arbitrary", "arbitrary"), vmem_limit_bytes=VMEM_LIMIT),
        name="in_proj",
    )(x2, attn_g, w_in, q_g, k_g, pool_w, pool_scale, *side)
    return outs[0], outs[1], outs[2:]


def _nt_dot(a, b):
    return lax.dot_general(a, b, (((1,), (1,)), ((), ())), preferred_element_type=F32)


def _attn_kernel(rb_ref, q_ref, k_ref, vt_ref, bias_ref, o_ref, kmean_sc, sel_sc, m_sc, acc_sc,
                 *, nb, heads):
    qb = pl.program_id(1)
    blk = MOBA_BLOCK

    @pl.when(qb == 0)
    def _():
        for n in range(nb):
            kn = k_ref[n * blk:(n + 1) * blk, :].astype(F32)
            kmean_sc[n:n + 1, :] = jnp.mean(kn, axis=0, keepdims=True)

    n_iota = lax.broadcasted_iota(jnp.int32, (nb, blk), 0)
    past = n_iota < qb
    start = pl.multiple_of(qb * blk, blk)
    head_slices = [slice(h * HEAD_DIM, (h + 1) * HEAD_DIM) for h in range(heads)]
    ones = jnp.ones((ONES_ROWS, blk), BF16)

    def values(n, hs):
        return jnp.concatenate([vt_ref[n, hs, :], ones], axis=0)


    prods = []
    for hs in head_slices:
        km = kmean_sc[:, hs]
        km_hi = km.astype(BF16)
        km_lo = (km - km_hi.astype(F32)).astype(BF16)
        lhs = jnp.concatenate([km_hi, km_lo, k_ref[pl.ds(start, blk), hs]], axis=0)
        prods.append(_nt_dot(lhs, q_ref[:, hs]))
    probs = []
    for h, hs in enumerate(head_slices):
        g = jnp.where(past, prods[h][:nb] + prods[h][nb:2 * nb], NEG)
        rank = jnp.zeros((nb, blk), jnp.int32)
        for m in range(nb):
            gm = g[m:m + 1, :]
            beats = (gm > g) | ((gm == g) & (n_iota > m))
            rank = rank + beats.astype(jnp.int32)
        sel_sc[h] = ((rank < MOBA_TOPK) & past).astype(F32)
        s = prods[h][2 * nb:] + bias_ref[h, 0]
        m0 = jnp.max(s, axis=0, keepdims=True)
        m_sc[h] = m0
        probs.append(jnp.exp2(s - m0).astype(BF16))
    for h, hs in enumerate(head_slices):
        acc_sc[h] = jnp.dot(values(qb, hs), probs[h], preferred_element_type=F32)

    def past_blocks(n0, count, dists):
        st = pl.multiple_of(n0 * blk, blk)
        prods = [_nt_dot(k_ref[pl.ds(st, count * blk), hs], q_ref[:, hs]) for hs in head_slices]
        scaled = []
        for h, s_all in enumerate(prods):
            far_bias = rb_ref[h, REL_BUCKETS - 1] * LOG2E
            tiles, m_tile = [], None
            for j, dist in enumerate(dists):
                s = s_all[j * blk:(j + 1) * blk]
                if dist is None:
                    c = far_bias
                else:
                    s = s + bias_ref[h, dist]
                    c = 0.0
                chosen = sel_sc[h, pl.ds(n0 + j, 1), :] > 0.5
                m_j = jnp.where(chosen, jnp.max(s, axis=0, keepdims=True) + c, NEG)
                m_tile = m_j if m_tile is None else jnp.maximum(m_tile, m_j)
                tiles.append((s, chosen, c))
            m_run = m_sc[h]
            m_new = jnp.maximum(m_run, m_tile)
            m_sc[h] = m_new
            p = [jnp.exp2(s - jnp.where(chosen, m_new - c, -NEG)).astype(BF16) for s, chosen, c in tiles]
            scaled.append((jnp.exp2(m_run - m_new), jnp.concatenate(p, axis=0)))
        for h, hs in enumerate(head_slices):
            alpha, p = scaled[h]
            vals = jnp.concatenate([values(n0 + j, hs) for j in range(count)], axis=1)
            acc_sc[h] = alpha * acc_sc[h] + jnp.dot(vals, p, preferred_element_type=F32)

    def far_group(i, carry):
        past_blocks(i * FAR_GROUP, FAR_GROUP, (None,) * FAR_GROUP)
        return carry

    def far_single(n, carry):
        past_blocks(n, 1, (None,))
        return carry

    def near_single(n, carry):
        past_blocks(n, 1, (qb - n,))
        return carry

    near_count = NEAR_DIST - 1
    far_end = jnp.maximum(qb - near_count, 0)
    far_groups = far_end // FAR_GROUP
    lax.fori_loop(0, far_groups, far_group, 0)
    lax.fori_loop(far_groups * FAR_GROUP, far_end, far_single, 0)
    lax.fori_loop(0, jnp.where(qb < near_count, qb, 0), near_single, 0)

    @pl.when(qb >= near_count)
    def _():
        past_blocks(qb - near_count, near_count, tuple(range(near_count, 0, -1)))

    for h, hs in enumerate(head_slices):
        acc = acc_sc[h]
        o_ref[:, hs] = (acc[:HEAD_DIM] / acc[HEAD_DIM:HEAD_DIM + 1]).T.astype(BF16)


def _moba_attention(rel_bias, qkm, vt, bias_tab, *, batch, seq, heads):
    tokens = qkm.shape[0]
    nb = seq // MOBA_BLOCK
    width = heads * HEAD_DIM
    assert seq % MOBA_BLOCK == 0
    kern = functools.partial(_attn_kernel, nb=nb, heads=heads)
    return pl.pallas_call(
        kern,
        grid=(batch, nb),
        in_specs=[
            pl.BlockSpec(memory_space=pltpu.SMEM),
            pl.BlockSpec((MOBA_BLOCK, width), lambda b, qb: (b * nb + qb, 0)),
            pl.BlockSpec((seq, width), lambda b, qb: (b, 1)),
            pl.BlockSpec((nb, width, MOBA_BLOCK), lambda b, qb: (b, 0, 0)),
            pl.BlockSpec((heads, NEAR_DIST, MOBA_BLOCK, MOBA_BLOCK), lambda b, qb: (0, 0, 0, 0),
                         pipeline_mode=pl.Buffered(1)),
        ],
        out_specs=pl.BlockSpec((MOBA_BLOCK, width), lambda b, qb: (b * nb + qb, 0)),
        out_shape=jax.ShapeDtypeStruct((tokens, width), BF16),
        scratch_shapes=[pltpu.VMEM((nb, width), F32),
                        pltpu.VMEM((heads, nb, MOBA_BLOCK), F32),
                        pltpu.VMEM((heads, 1, MOBA_BLOCK), F32),
                        pltpu.VMEM((heads, HEAD_DIM + ONES_ROWS, MOBA_BLOCK), F32)],
        compiler_params=pltpu.CompilerParams(
            dimension_semantics=("arbitrary", "arbitrary"), vmem_limit_bytes=VMEM_LIMIT),
        name="moba_attn",
    )(rel_bias, qkm, qkm, vt, bias_tab)


def _outproj_kernel(x_ref, a_ref, m_ref, wa_ref, wm_ref, o_ref):
    o_ref[...] = (x_ref[...]
                  + jnp.dot(a_ref[...], wa_ref[...], preferred_element_type=F32)
                  + jnp.dot(m_ref[...], wm_ref[...], preferred_element_type=F32))


def _out_proj(x2, a, qkm, w_out, *, tm):
    tokens, d_model = x2.shape
    half = a.shape[1]
    m_col = qkm.shape[1] // half - 1
    return pl.pallas_call(
        _outproj_kernel,
        grid=(tokens // tm,),
        in_specs=[
            pl.BlockSpec((tm, d_model), lambda i: (i, 0)),
            pl.BlockSpec((tm, half), lambda i: (i, 0)),
            pl.BlockSpec((tm, half), lambda i: (i, m_col)),
            pl.BlockSpec((half, d_model), lambda i: (0, 0)),
            pl.BlockSpec((half, d_model), lambda i: (1, 0)),
        ],
        out_specs=pl.BlockSpec((tm, d_model), lambda i: (i, 0)),
        out_shape=jax.ShapeDtypeStruct((tokens, d_model), F32),
        compiler_params=pltpu.CompilerParams(
            dimension_semantics=("arbitrary",), vmem_limit_bytes=VMEM_LIMIT),
        name="out_proj",
    )(x2, a, qkm, w_out, w_out)


def _ffn_kernel(x_ref, g_ref, wug_ref, wuv_ref, cwg_ref, cwv_ref, cbg_ref, cbv_ref, wd_ref,
                o_ref, h_sc, carry_sc, *, tm, tiles_per_seq):
    i = pl.program_id(0)
    f = pl.program_id(1)

    @pl.when(f == 0)
    def _():
        x = x_ref[...]
        h_sc[...] = _rms(x, g_ref[...]).astype(BF16)
        o_ref[...] = x

    first = (i % tiles_per_seq) == 0
    h = h_sc[...]

    def conv(w_ref, cw_ref, cb_ref, slot):
        u = jnp.dot(h, w_ref[...], preferred_element_type=F32)
        prev = jnp.where(first, 0.0, carry_sc[slot, f])
        carry_sc[slot, f] = u[tm - CONV_HALO:, :]
        e = jnp.concatenate([prev, u], axis=0)
        y = cb_ref[...]
        for tap in range(CONV_WIDTH - 1):
            y = y + cw_ref[tap:tap + 1, :] * pltpu.roll(e, CONV_WIDTH - 1 - tap, axis=0)[CONV_HALO:, :]
        return y + cw_ref[CONV_WIDTH - 1:CONV_WIDTH, :] * u

    yg = conv(wug_ref, cwg_ref, cbg_ref, 0)
    yv = conv(wuv_ref, cwv_ref, cbv_ref, 1)
    act = yg / (1.0 + jnp.exp2(yg * -LOG2E)) * yv
    o_ref[...] += jnp.dot(act.astype(BF16), wd_ref[...], preferred_element_type=F32)


def _conv_ffn(x1, ffn_g, w_up, conv_w, conv_b, w_down, *, seq, tm, tf):
    tokens, d_model = x1.shape
    d_ff = w_down.shape[0]
    nf = d_ff // tf
    assert d_ff % tf == 0 and seq % tm == 0 and CONV_WIDTH - 1 <= CONV_HALO <= tm
    kern = functools.partial(_ffn_kernel, tm=tm, tiles_per_seq=seq // tm)
    return pl.pallas_call(
        kern,
        grid=(tokens // tm, nf),
        in_specs=[
            pl.BlockSpec((tm, d_model), lambda i, f: (i, 0)),
            pl.BlockSpec((1, d_model), lambda i, f: (0, 0)),
            pl.BlockSpec((d_model, tf), lambda i, f: (0, f)),
            pl.BlockSpec((d_model, tf), lambda i, f: (0, nf + f)),
            pl.BlockSpec((CONV_WIDTH, tf), lambda i, f: (0, f)),
            pl.BlockSpec((CONV_WIDTH, tf), lambda i, f: (0, nf + f)),
            pl.BlockSpec((1, tf), lambda i, f: (0, f)),
            pl.BlockSpec((1, tf), lambda i, f: (0, nf + f)),
            pl.BlockSpec((tf, d_model), lambda i, f: (f, 0)),
        ],
        out_specs=pl.BlockSpec((tm, d_model), lambda i, f: (i, 0)),
        out_shape=jax.ShapeDtypeStruct((tokens, d_model), F32),
        scratch_shapes=[pltpu.VMEM((tm, d_model), BF16),
                        pltpu.VMEM((2, nf, CONV_HALO, tf), F32)],
        compiler_params=pltpu.CompilerParams(
            dimension_semantics=("arbitrary", "arbitrary"), vmem_limit_bytes=VMEM_LIMIT),
        name="conv_ffn",
    )(x1, ffn_g, w_up, w_up, conv_w, conv_w, conv_b, conv_b, w_down)


def kernel(x, attn_norm_g, w_in, q_norm_g, k_norm_g, rel_bias, pool_w, pool_scale, w_out,
           ffn_norm_g, w_up, conv_w, conv_b, w_down):
    batch, seq, d_model = x.shape
    depth = w_in.shape[0]
    heads = rel_bias.shape[0]
    x2 = x.reshape(batch * seq, d_model)
    bias_tab = _bias_tables(rel_bias)
    for l in range(depth):
        qkm, vt, (w_out_l, w_up_l, w_down_l) = _in_proj(
            x2, attn_norm_g[l][None], w_in[l].astype(BF16), q_norm_g[l][None], k_norm_g[l][None],
            pool_w[l].astype(BF16), pool_scale[l][None], (w_out, w_up, w_down), l, seq=seq, tm=1024)
        a = _moba_attention(rel_bias, qkm, vt, bias_tab, batch=batch, seq=seq, heads=heads)
        x1 = _out_proj(x2, a, qkm, w_out_l, tm=512)
        x2 = _conv_ffn(x1, ffn_norm_g[l][None], w_up_l, conv_w[l], conv_b[l][None], w_down_l,
                       seq=seq, tm=512, tf=512)
    return x2.reshape(batch, seq, d_model)
```

```python
import functools
import math

import jax
import jax.numpy as jnp
from jax import lax
from jax.experimental import pallas as pl
from jax.experimental.pallas import tpu as pltpu

HEAD_DIM = 128
MOBA_BLOCK = 256
MOBA_TOPK = 3
POOL_WINDOWS = (2, 4, 8, 16)
REL_BUCKETS = 32
REL_MAX_DIST = 1024
CONV_WIDTH = 3
EPS = 1e-6
NEG = -1e30

POOL_HALO = 16
CONV_HALO = 8
NEAR_DIST = 5

ONES_ROWS = 16
LOG2E = math.log2(math.e)
FAR_GROUP = 2

VMEM_LIMIT = 56 * 1024 * 1024

BF16 = jnp.bfloat16
F32 = jnp.float32
BF16_SUBLANES = 16


def _rel_bucket_thresholds():
    max_exact = REL_BUCKETS // 2
    span = REL_BUCKETS - max_exact
    out = []
    for k in range(1, span):
        edge = max_exact * (REL_MAX_DIST / max_exact) ** (k / span)
        r = round(edge)
        out.append(r if abs(edge - r) < 1e-9 else math.ceil(edge))
    return tuple(out)


def _bias_kernel(rb_ref, o_ref):
    h = pl.program_id(0)
    blk = MOBA_BLOCK
    shape = (8, 2 * blk)
    m = lax.broadcasted_iota(jnp.int32, shape, 1)
    offset = jnp.where(m < blk, m, m - 2 * blk)
    max_exact = REL_BUCKETS // 2
    for d in range(NEAR_DIST):
        n = jnp.maximum(d * blk + offset, 0)
        large = jnp.full(shape, max_exact, jnp.int32)
        for t in _rel_bucket_thresholds():
            large = large + (n >= t).astype(jnp.int32)
        bucket = jnp.where(n < max_exact, n, large)
        val = jnp.zeros(shape, F32)
        for b in range(REL_BUCKETS):
            val = jnp.where(bucket == b, rb_ref[h, b], val)
        val = val * LOG2E
        if d == 0:
            val = jnp.where(offset < 0, NEG, val)
        strip = jnp.broadcast_to(val[0:1, :], (blk, 2 * blk))
        o_ref[0, d] = pltpu.roll(strip, 0, axis=1, stride=1, stride_axis=0)[:, :blk]


def _bias_tables(rel_bias):
    assert (NEAR_DIST - 1) * MOBA_BLOCK + 1 >= _rel_bucket_thresholds()[-1]
    heads = rel_bias.shape[0]
    return pl.pallas_call(
        _bias_kernel,
        grid=(heads,),
        in_specs=[pl.BlockSpec(memory_space=pltpu.SMEM)],
        out_specs=pl.BlockSpec((1, NEAR_DIST, MOBA_BLOCK, MOBA_BLOCK), lambda h: (h, 0, 0, 0)),
        out_shape=jax.ShapeDtypeStruct((heads, NEAR_DIST, MOBA_BLOCK, MOBA_BLOCK), F32),
        name="bias_tables",
    )(rel_bias)


def _rms(y, g):
    ms = jnp.mean(y * y, axis=-1, keepdims=True)
    return y * lax.rsqrt(ms + EPS) * g


def _inproj_kernel(x_ref, g_ref, w_ref, qg_ref, kg_ref, pw_ref, ps_ref, *rest,
                   tm, tiles_per_seq, heads, group_width, n_side):
    side_in, (o_ref, vt_ref), rest = rest[:n_side], rest[n_side:n_side + 2], rest[n_side + 2:]
    side_out, (h_sc, halo_sc) = rest[:n_side], rest[n_side:]
    i = pl.program_id(0)
    j = pl.program_id(1)

    for src, dst in zip(side_in, side_out):
        dst[...] = src[...].astype(BF16)

    @pl.when(j == 0)
    def _():
        h_sc[...] = _rms(x_ref[...], g_ref[...]).astype(BF16)

    y = jnp.dot(h_sc[...], w_ref[...], preferred_element_type=F32)

    def head_norm(g):
        for h in range(heads):
            sl = slice(h * HEAD_DIM, (h + 1) * HEAD_DIM)
            o_ref[:, sl] = _rms(y[:, sl], g).astype(BF16)

    @pl.when(j == 0)
    def _():
        head_norm(qg_ref[...] * (HEAD_DIM ** -0.5 * LOG2E))

    @pl.when(j == 1)
    def _():
        head_norm(kg_ref[...])

    @pl.when(j == 2)
    def _():
        for r in range(tm // MOBA_BLOCK):
            vt_ref[r] = y[r * MOBA_BLOCK:(r + 1) * MOBA_BLOCK, :].T.astype(BF16)

    @pl.when(j == 3)
    def _():
        seq_tile = i % tiles_per_seq
        halo = jnp.where(seq_tile == 0, 0.0, halo_sc[...])
        halo_sc[...] = y[tm - POOL_HALO:, :]
        t = seq_tile * tm + lax.broadcasted_iota(jnp.int32, (tm, 1), 0)
        for g, w in enumerate(POOL_WINDOWS):
            sl = slice(g * group_width, (g + 1) * group_width)
            pg = y[:, sl]
            a = jnp.concatenate([halo[:, sl], pg], axis=0)
            shift = 1
            while shift < w:
                a = a + pltpu.roll(a, shift, axis=0)
                shift *= 2
            count = jnp.minimum(t + 1, w).astype(F32)
            mixed = a[POOL_HALO:, :] / count - pg
            mg = jnp.dot(mixed.astype(BF16), pw_ref[g], preferred_element_type=F32)
            o_ref[:, sl] = (mg * ps_ref[:, sl]).astype(BF16)


def _in_proj(x2, attn_g, w_in, q_g, k_g, pool_w, pool_scale, side, layer, *, seq, tm):
    tokens, d_model = x2.shape
    in_width = w_in.shape[1]
    tn = in_width // 4
    heads = tn // HEAD_DIM
    groups, group_width, _ = pool_w.shape
    assert groups == len(POOL_WINDOWS) and groups * group_width == tn
    assert seq % tm == 0 and tm >= POOL_HALO and tm % MOBA_BLOCK == 0
    kern = functools.partial(_inproj_kernel, tm=tm, tiles_per_seq=seq // tm, heads=heads,
                             group_width=group_width, n_side=len(side))
    out_col = lambda i, j: (i, jnp.minimum(j, 1) + j // 3)
    steps = (tokens // tm) * 4
    slab_in, slab_out = [], []
    for a in side:
        rows = a.shape[1]
        hold = next(h for h in (1, 2, 4, 8) if steps % h == 0 and rows % (steps // h * BF16_SUBLANES) == 0)
        slab = (rows // (steps // hold), a.shape[2])
        slab_in.append(pl.BlockSpec((None,) + slab, lambda i, j, hold=hold: (layer, (i * 4 + j) // hold, 0)))
        slab_out.append(pl.BlockSpec(slab, lambda i, j, hold=hold: ((i * 4 + j) // hold, 0)))
    outs = pl.pallas_call(
        kern,
        grid=(tokens // tm, 4),
        in_specs=[
            pl.BlockSpec((tm, d_model), lambda i, j: (i, 0)),
            pl.BlockSpec((1, d_model), lambda i, j: (0, 0)),
            pl.BlockSpec((d_model, tn), lambda i, j: (0, j)),
            pl.BlockSpec((1, HEAD_DIM), lambda i, j: (0, 0)),
            pl.BlockSpec((1, HEAD_DIM), lambda i, j: (0, 0)),
            pl.BlockSpec((groups, group_width, group_width), lambda i, j: (0, 0, 0)),
            pl.BlockSpec((1, tn), lambda i, j: (0, 0)),
        ] + slab_in,
        out_specs=[pl.BlockSpec((tm, tn), out_col),
                   pl.BlockSpec((tm // MOBA_BLOCK, tn, MOBA_BLOCK), lambda i, j: (i, 0, 0))] + slab_out,
        out_shape=[jax.ShapeDtypeStruct((tokens, 3 * tn), BF16),
                   jax.ShapeDtypeStruct((tokens // MOBA_BLOCK, tn, MOBA_BLOCK), BF16)]
        + [jax.ShapeDtypeStruct(a.shape[1:], BF16) for a in side],
        scratch_shapes=[pltpu.VMEM((tm, d_model), BF16), pltpu.VMEM((POOL_HALO, tn), F32)],
        compiler_params=pltpu.CompilerParams(
            dimension_semantics=("arbitrary", "arbitrary"), vmem_limit_bytes=VMEM_LIMIT),
        name="in_proj",
    )(x2, attn_g, w_in, q_g, k_g, pool_w, pool_scale, *side)
    return outs[0], outs[1], outs[2:]


def _nt_dot(a, b):
    return lax.dot_general(a, b, (((1,), (1,)), ((), ())), preferred_element_type=F32)


def _attn_kernel(rb_ref, q_ref, k_ref, vt_ref, bias_ref, o_ref, kmean_sc, sel_sc, m_sc, acc_sc,
                 *, nb, heads):
    qb = pl.program_id(1)
    blk = MOBA_BLOCK

    @pl.when(qb == 0)
    def _():
        for n in range(nb):
            kn = k_ref[n * blk:(n + 1) * blk, :].astype(F32)
            kmean_sc[n:n + 1, :] = jnp.mean(kn, axis=0, keepdims=True)

    n_iota = lax.broadcasted_iota(jnp.int32, (nb, blk), 0)
    past = n_iota < qb
    start = pl.multiple_of(qb * blk, blk)
    head_slices = [slice(h * HEAD_DIM, (h + 1) * HEAD_DIM) for h in range(heads)]
    ones = jnp.ones((ONES_ROWS, blk), BF16)

    def values(n, hs):
        return jnp.concatenate([vt_ref[n, hs, :], ones], axis=0)


    prods = []
    for hs in head_slices:
        km = kmean_sc[:, hs]
        km_hi = km.astype(BF16)
        km_lo = (km - km_hi.astype(F32)).astype(BF16)
        lhs = jnp.concatenate([km_hi, km_lo, k_ref[pl.ds(start, blk), hs]], axis=0)
        prods.append(_nt_dot(lhs, q_ref[:, hs]))
    probs = []
    for h, hs in enumerate(head_slices):
        g = jnp.where(past, prods[h][:nb] + prods[h][nb:2 * nb], NEG)
        rank = jnp.zeros((nb, blk), jnp.int32)
        for m in range(nb):
            gm = g[m:m + 1, :]
            beats = (gm > g) | ((gm == g) & (n_iota > m))
            rank = rank + beats.astype(jnp.int32)
        sel_sc[h] = ((rank < MOBA_TOPK) & past).astype(F32)
        s = prods[h][2 * nb:] + bias_ref[h, 0]
        m0 = jnp.max(s, axis=0, keepdims=True)
        m_sc[h] = m0
        probs.append(jnp.exp2(s - m0).astype(BF16))
    for h, hs in enumerate(head_slices):
        acc_sc[h] = jnp.dot(values(qb, hs), probs[h], preferred_element_type=F32)

    def past_blocks(n0, count, dists):
        st = pl.multiple_of(n0 * blk, blk)
        prods = [_nt_dot(k_ref[pl.ds(st, count * blk), hs], q_ref[:, hs]) for hs in head_slices]
        scaled = []
        for h, s_all in enumerate(prods):
            far_bias = rb_ref[h, REL_BUCKETS - 1] * LOG2E
            tiles, m_tile = [], None
            for j, dist in enumerate(dists):
                s = s_all[j * blk:(j + 1) * blk]
                if dist is None:
                    c = far_bias
                else:
                    s = s + bias_ref[h, dist]
                    c = 0.0
                chosen = sel_sc[h, pl.ds(n0 + j, 1), :] > 0.5
                m_j = jnp.where(chosen, jnp.max(s, axis=0, keepdims=True) + c, NEG)
                m_tile = m_j if m_tile is None else jnp.maximum(m_tile, m_j)
                tiles.append((s, chosen, c))
            m_run = m_sc[h]
            m_new = jnp.maximum(m_run, m_tile)
            m_sc[h] = m_new
            p = [jnp.exp2(s - jnp.where(chosen, m_new - c, -NEG)).astype(BF16) for s, chosen, c in tiles]
            scaled.append((jnp.exp2(m_run - m_new), jnp.concatenate(p, axis=0)))
        for h, hs in enumerate(head_slices):
            alpha, p = scaled[h]
            vals = jnp.concatenate([values(n0 + j, hs) for j in range(count)], axis=1)
            acc_sc[h] = alpha * acc_sc[h] + jnp.dot(vals, p, preferred_element_type=F32)

    def far_group(i, carry):
        past_blocks(i * FAR_GROUP, FAR_GROUP, (None,) * FAR_GROUP)
        return carry

    def far_single(n, carry):
        past_blocks(n, 1, (None,))
        return carry

    def near_single(n, carry):
        past_blocks(n, 1, (qb - n,))
        return carry

    near_count = NEAR_DIST - 1
    far_end = jnp.maximum(qb - near_count, 0)
    far_groups = far_end // FAR_GROUP
    lax.fori_loop(0, far_groups, far_group, 0)
    lax.fori_loop(far_groups * FAR_GROUP, far_end, far_single, 0)
    lax.fori_loop(0, jnp.where(qb < near_count, qb, 0), near_single, 0)

    @pl.when(qb >= near_count)
    def _():
        past_blocks(qb - near_count, near_count, tuple(range(near_count, 0, -1)))

    for h, hs in enumerate(head_slices):
        acc = acc_sc[h]
        o_ref[:, hs] = (acc[:HEAD_DIM] / acc[HEAD_DIM:HEAD_DIM + 1]).T.astype(BF16)


def _moba_attention(rel_bias, qkm, vt, bias_tab, *, batch, seq, heads):
    tokens = qkm.shape[0]
    nb = seq // MOBA_BLOCK
    width = heads * HEAD_DIM
    assert seq % MOBA_BLOCK == 0
    kern = functools.partial(_attn_kernel, nb=nb, heads=heads)
    return pl.pallas_call(
        kern,
        grid=(batch, nb),
        in_specs=[
            pl.BlockSpec(memory_space=pltpu.SMEM),
            pl.BlockSpec((MOBA_BLOCK, width), lambda b, qb: (b * nb + qb, 0)),
            pl.BlockSpec((seq, width), lambda b, qb: (b, 1)),
            pl.BlockSpec((nb, width, MOBA_BLOCK), lambda b, qb: (b, 0, 0)),
            pl.BlockSpec((heads, NEAR_DIST, MOBA_BLOCK, MOBA_BLOCK), lambda b, qb: (0, 0, 0, 0),
                         pipeline_mode=pl.Buffered(1)),
        ],
        out_specs=pl.BlockSpec((MOBA_BLOCK, width), lambda b, qb: (b * nb + qb, 0)),
        out_shape=jax.ShapeDtypeStruct((tokens, width), BF16),
        scratch_shapes=[pltpu.VMEM((nb, width), F32),
                        pltpu.VMEM((heads, nb, MOBA_BLOCK), F32),
                        pltpu.VMEM((heads, 1, MOBA_BLOCK), F32),
                        pltpu.VMEM((heads, HEAD_DIM + ONES_ROWS, MOBA_BLOCK), F32)],
        compiler_params=pltpu.CompilerParams(
            dimension_semantics=("arbitrary", "arbitrary"), vmem_limit_bytes=VMEM_LIMIT),
        name="moba_attn",
    )(rel_bias, qkm, qkm, vt, bias_tab)


def _outproj_kernel(x_ref, a_ref, m_ref, wa_ref, wm_ref, o_ref):
    o_ref[...] = (x_ref[...]
                  + jnp.dot(a_ref[...], wa_ref[...], preferred_element_type=F32)
                  + jnp.dot(m_ref[...], wm_ref[...], preferred_element_type=F32))


def _out_proj(x2, a, qkm, w_out, *, tm):
    tokens, d_model = x2.shape
    half = a.shape[1]
    m_col = qkm.shape[1] // half - 1
    return pl.pallas_call(
        _outproj_kernel,
        grid=(tokens // tm,),
        in_specs=[
            pl.BlockSpec((tm, d_model), lambda i: (i, 0)),
            pl.BlockSpec((tm, half), lambda i: (i, 0)),
            pl.BlockSpec((tm, half), lambda i: (i, m_col)),
            pl.BlockSpec((half, d_model), lambda i: (0, 0)),
            pl.BlockSpec((half, d_model), lambda i: (1, 0)),
        ],
        out_specs=pl.BlockSpec((tm, d_model), lambda i: (i, 0)),
        out_shape=jax.ShapeDtypeStruct((tokens, d_model), F32),
        compiler_params=pltpu.CompilerParams(
            dimension_semantics=("arbitrary",), vmem_limit_bytes=VMEM_LIMIT),
        name="out_proj",
    )(x2, a, qkm, w_out, w_out)


def _ffn_kernel(x_ref, g_ref, wug_ref, wuv_ref, cwg_ref, cwv_ref, cbg_ref, cbv_ref, wd_ref,
                o_ref, h_sc, carry_sc, *, tm, tiles_per_seq):
    i = pl.program_id(0)
    f = pl.program_id(1)

    first = (i % tiles_per_seq) == 0

    def conv(h, w_ref, cw_ref, cb_ref, slot):
        u = jnp.dot(h, w_ref[...], preferred_element_type=F32)
        prev = jnp.where(first, 0.0, carry_sc[slot, f])
        carry_sc[slot, f] = u[tm - CONV_HALO:, :]
        e = jnp.concatenate([prev, u], axis=0)
        y = cb_ref[...]
        for tap in range(CONV_WIDTH - 1):
            y = y + cw_ref[tap:tap + 1, :] * pltpu.roll(e, CONV_WIDTH - 1 - tap, axis=0)[CONV_HALO:, :]
        return y + cw_ref[CONV_WIDTH - 1:CONV_WIDTH, :] * u

    def down_projected():
        h = h_sc[...]
        yg = conv(h, wug_ref, cwg_ref, cbg_ref, 0)
        yv = conv(h, wuv_ref, cwv_ref, cbv_ref, 1)
        act = yg / (1.0 + jnp.exp2(yg * -LOG2E)) * yv
        return jnp.dot(act.astype(BF16), wd_ref[...], preferred_element_type=F32)

    @pl.when(f == 0)
    def _():
        x = x_ref[...]
        h_sc[...] = _rms(x, g_ref[...]).astype(BF16)
        o_ref[...] = x + down_projected()

    @pl.when(f > 0)
    def _():
        o_ref[...] += down_projected()


def _conv_ffn(x1, ffn_g, w_up, conv_w, conv_b, w_down, *, seq, tm, tf):
    tokens, d_model = x1.shape
    d_ff = w_down.shape[0]
    nf = d_ff // tf
    assert d_ff % tf == 0 and seq % tm == 0 and CONV_WIDTH - 1 <= CONV_HALO <= tm
    kern = functools.partial(_ffn_kernel, tm=tm, tiles_per_seq=seq // tm)
    return pl.pallas_call(
        kern,
        grid=(tokens // tm, nf),
        in_specs=[
            pl.BlockSpec((tm, d_model), lambda i, f: (i, 0)),
            pl.BlockSpec((1, d_model), lambda i, f: (0, 0)),
            pl.BlockSpec((d_model, tf), lambda i, f: (0, f)),
            pl.BlockSpec((d_model, tf), lambda i, f: (0, nf + f)),
            pl.BlockSpec((CONV_WIDTH, tf), lambda i, f: (0, f)),
            pl.BlockSpec((CONV_WIDTH, tf), lambda i, f: (0, nf + f)),
            pl.BlockSpec((1, tf), lambda i, f: (0, f)),
            pl.BlockSpec((1, tf), lambda i, f: (0, nf + f)),
            pl.BlockSpec((tf, d_model), lambda i, f: (f, 0)),
        ],
        out_specs=pl.BlockSpec((tm, d_model), lambda i, f: (i, 0)),
        out_shape=jax.ShapeDtypeStruct((tokens, d_model), F32),
        scratch_shapes=[pltpu.VMEM((tm, d_model), BF16),
                        pltpu.VMEM((2, nf, CONV_HALO, tf), F32)],
        compiler_params=pltpu.CompilerParams(
            dimension_semantics=("arbitrary", "arbitrary"), vmem_limit_bytes=VMEM_LIMIT),
        name="conv_ffn",
    )(x1, ffn_g, w_up, w_up, conv_w, conv_w, conv_b, conv_b, w_down)


def kernel(x, attn_norm_g, w_in, q_norm_g, k_norm_g, rel_bias, pool_w, pool_scale, w_out,
           ffn_norm_g, w_up, conv_w, conv_b, w_down):
    batch, seq, d_model = x.shape
    depth = w_in.shape[0]
    heads = rel_bias.shape[0]
    x2 = x.reshape(batch * seq, d_model)
    bias_tab = _bias_tables(rel_bias)
    for l in range(depth):
        qkm, vt, (w_out_l, w_up_l, w_down_l) = _in_proj(
            x2, attn_norm_g[l][None], w_in[l].astype(BF16), q_norm_g[l][None], k_norm_g[l][None],
            pool_w[l].astype(BF16), pool_scale[l][None], (w_out, w_up, w_down), l, seq=seq, tm=1024)
        a = _moba_attention(rel_bias, qkm, vt, bias_tab, batch=batch, seq=seq, heads=heads)
        x1 = _out_proj(x2, a, qkm, w_out_l, tm=512)
        x2 = _conv_ffn(x1, ffn_norm_g[l][None], w_up_l, conv_w[l], conv_b[l][None], w_down_l,
                       seq=seq, tm=512, tf=512)
    return x2.reshape(batch, seq, d_model)
```

```python
import functools
import math

import jax
import jax.numpy as jnp
from jax import lax
from jax.experimental import pallas as pl
from jax.experimental.pallas import tpu as pltpu

HEAD_DIM = 128
MOBA_BLOCK = 256
MOBA_TOPK = 3
POOL_WINDOWS = (2, 4, 8, 16)
REL_BUCKETS = 32
REL_MAX_DIST = 1024
CONV_WIDTH = 3
EPS = 1e-6
NEG = -1e30

POOL_HALO = 16
CONV_HALO = 8
NEAR_DIST = 5

ONES_ROWS = 16
LOG2E = math.log2(math.e)
FAR_GROUP = 2

VMEM_LIMIT = 56 * 1024 * 1024

BF16 = jnp.bfloat16
F32 = jnp.float32
BF16_SUBLANES = 16
FFN_K_BLOCK = 256
FFN_ROW_CHUNKS = 8
FFN_COL_CHUNKS = 2
ANCHOR_LANES = 128
ANCHOR_LAG = 2


def _rel_bucket_thresholds():
    max_exact = REL_BUCKETS // 2
    span = REL_BUCKETS - max_exact
    out = []
    for k in range(1, span):
        edge = max_exact * (REL_MAX_DIST / max_exact) ** (k / span)
        r = round(edge)
        out.append(r if abs(edge - r) < 1e-9 else math.ceil(edge))
    return tuple(out)


def _bias_kernel(rb_ref, o_ref):
    h = pl.program_id(0)
    blk = MOBA_BLOCK
    shape = (8, 2 * blk)
    m = lax.broadcasted_iota(jnp.int32, shape, 1)
    offset = jnp.where(m < blk, m, m - 2 * blk)
    max_exact = REL_BUCKETS // 2
    for d in range(NEAR_DIST):
        n = jnp.maximum(d * blk + offset, 0)
        large = jnp.full(shape, max_exact, jnp.int32)
        for t in _rel_bucket_thresholds():
            large = large + (n >= t).astype(jnp.int32)
        bucket = jnp.where(n < max_exact, n, large)
        val = jnp.zeros(shape, F32)
        for b in range(REL_BUCKETS):
            val = jnp.where(bucket == b, rb_ref[h, b], val)
        val = val * LOG2E
        if d == 0:
            val = jnp.where(offset < 0, NEG, val)
        strip = jnp.broadcast_to(val[0:1, :], (blk, 2 * blk))
        o_ref[0, d] = pltpu.roll(strip, 0, axis=1, stride=1, stride_axis=0)[:, :blk]


def _bias_tables(rel_bias):
    assert (NEAR_DIST - 1) * MOBA_BLOCK + 1 >= _rel_bucket_thresholds()[-1]
    heads = rel_bias.shape[0]
    return pl.pallas_call(
        _bias_kernel,
        grid=(heads,),
        in_specs=[pl.BlockSpec(memory_space=pltpu.SMEM)],
        out_specs=pl.BlockSpec((1, NEAR_DIST, MOBA_BLOCK, MOBA_BLOCK), lambda h: (h, 0, 0, 0)),
        out_shape=jax.ShapeDtypeStruct((heads, NEAR_DIST, MOBA_BLOCK, MOBA_BLOCK), F32),
        name="bias_tables",
    )(rel_bias)


def _rms(y, g):
    ms = jnp.mean(y * y, axis=-1, keepdims=True)
    return y * lax.rsqrt(ms + EPS) * g


def _inproj_kernel(x_ref, g_ref, w_ref, qg_ref, kg_ref, pw_ref, ps_ref, *rest,
                   tm, tiles_per_seq, heads, group_width, n_side):
    side_in, (o_ref, vt_ref), rest = rest[:n_side], rest[n_side:n_side + 2], rest[n_side + 2:]
    side_out, (h_sc, halo_sc) = rest[:n_side], rest[n_side:]
    i = pl.program_id(0)
    j = pl.program_id(1)

    for src, dst in zip(side_in, side_out):
        dst[...] = src[...].astype(BF16)

    @pl.when(j == 0)
    def _():
        h_sc[...] = _rms(x_ref[...], g_ref[...]).astype(BF16)

    y = jnp.dot(h_sc[...], w_ref[...], preferred_element_type=F32)

    def head_norm(g):
        for h in range(heads):
            sl = slice(h * HEAD_DIM, (h + 1) * HEAD_DIM)
            o_ref[:, sl] = _rms(y[:, sl], g).astype(BF16)

    @pl.when(j == 0)
    def _():
        head_norm(qg_ref[...] * (HEAD_DIM ** -0.5 * LOG2E))

    @pl.when(j == 1)
    def _():
        head_norm(kg_ref[...])

    @pl.when(j == 2)
    def _():
        for r in range(tm // MOBA_BLOCK):
            vt_ref[r] = y[r * MOBA_BLOCK:(r + 1) * MOBA_BLOCK, :].T.astype(BF16)

    @pl.when(j == 3)
    def _():
        seq_tile = i % tiles_per_seq
        halo = jnp.where(seq_tile == 0, 0.0, halo_sc[...])
        halo_sc[...] = y[tm - POOL_HALO:, :]
        t = seq_tile * tm + lax.broadcasted_iota(jnp.int32, (tm, 1), 0)
        for g, w in enumerate(POOL_WINDOWS):
            sl = slice(g * group_width, (g + 1) * group_width)
            pg = y[:, sl]
            a = jnp.concatenate([halo[:, sl], pg], axis=0)
            shift = 1
            while shift < w:
                a = a + pltpu.roll(a, shift, axis=0)
                shift *= 2
            count = jnp.minimum(t + 1, w).astype(F32)
            mixed = a[POOL_HALO:, :] / count - pg
            mg = jnp.dot(mixed.astype(BF16), pw_ref[g], preferred_element_type=F32)
            o_ref[:, sl] = (mg * ps_ref[:, sl]).astype(BF16)


def _in_proj(x2, attn_g, w_in, q_g, k_g, pool_w, pool_scale, side, layer, *, seq, tm):
    tokens, d_model = x2.shape
    in_width = w_in.shape[1]
    tn = in_width // 4
    heads = tn // HEAD_DIM
    groups, group_width, _ = pool_w.shape
    assert groups == len(POOL_WINDOWS) and groups * group_width == tn
    assert seq % tm == 0 and tm >= POOL_HALO and tm % MOBA_BLOCK == 0
    kern = functools.partial(_inproj_kernel, tm=tm, tiles_per_seq=seq // tm, heads=heads,
                             group_width=group_width, n_side=len(side))
    out_col = lambda i, j: (i, jnp.minimum(j, 1) + j // 3)
    steps = (tokens // tm) * 4
    slab_in, slab_out = [], []
    for a in side:
        rows = a.shape[1]
        hold = next(h for h in (1, 2, 4, 8) if steps % h == 0 and rows % (steps // h * BF16_SUBLANES) == 0)
        slab = (rows // (steps // hold), a.shape[2])
        slab_in.append(pl.BlockSpec((None,) + slab, lambda i, j, hold=hold: (layer, (i * 4 + j) // hold, 0)))
        slab_out.append(pl.BlockSpec(slab, lambda i, j, hold=hold: ((i * 4 + j) // hold, 0)))
    outs = pl.pallas_call(
        kern,
        grid=(tokens // tm, 4),
        in_specs=[
            pl.BlockSpec((tm, d_model), lambda i, j: (i, 0)),
            pl.BlockSpec((1, d_model), lambda i, j: (0, 0)),
            pl.BlockSpec((d_model, tn), lambda i, j: (0, j)),
            pl.BlockSpec((1, HEAD_DIM), lambda i, j: (0, 0)),
            pl.BlockSpec((1, HEAD_DIM), lambda i, j: (0, 0)),
            pl.BlockSpec((groups, group_width, group_width), lambda i, j: (0, 0, 0)),
            pl.BlockSpec((1, tn), lambda i, j: (0, 0)),
        ] + slab_in,
        out_specs=[pl.BlockSpec((tm, tn), out_col),
                   pl.BlockSpec((tm // MOBA_BLOCK, tn, MOBA_BLOCK), lambda i, j: (i, 0, 0))] + slab_out,
        out_shape=[jax.ShapeDtypeStruct((tokens, 3 * tn), BF16),
                   jax.ShapeDtypeStruct((tokens // MOBA_BLOCK, tn, MOBA_BLOCK), BF16)]
        + [jax.ShapeDtypeStruct(a.shape[1:], BF16) for a in side],
        scratch_shapes=[pltpu.VMEM((tm, d_model), BF16), pltpu.VMEM((POOL_HALO, tn), F32)],
        compiler_params=pltpu.CompilerParams(
            dimension_semantics=("arbitrary", "arbitrary"), vmem_limit_bytes=VMEM_LIMIT),
        name="in_proj",
    )(x2, attn_g, w_in, q_g, k_g, pool_w, pool_scale, *side)
    return outs[0], outs[1], outs[2:]


def _nt_dot(a, b):
    return lax.dot_general(a, b, (((1,), (1,)), ((), ())), preferred_element_type=F32)


def _attn_kernel(rb_ref, q_ref, k_ref, vt_ref, bias_ref, o_ref, kmean_sc, sel_sc, m_sc, acc_sc,
                 *, nb, heads):
    qb = pl.program_id(1)
    blk = MOBA_BLOCK

    @pl.when(qb == 0)
    def _():
        for n in range(nb):
            kn = k_ref[n * blk:(n + 1) * blk, :].astype(F32)
            kmean_sc[n:n + 1, :] = jnp.mean(kn, axis=0, keepdims=True)

    n_iota = lax.broadcasted_iota(jnp.int32, (nb, blk), 0)
    past = n_iota < qb
    start = pl.multiple_of(qb * blk, blk)
    head_slices = [slice(h * HEAD_DIM, (h + 1) * HEAD_DIM) for h in range(heads)]
    ones = jnp.ones((ONES_ROWS, blk), BF16)

    def values(n, hs):
        return jnp.concatenate([vt_ref[n, hs, :], ones], axis=0)


    prods = []
    for hs in head_slices:
        km = kmean_sc[:, hs]
        km_hi = km.astype(BF16)
        km_lo = (km - km_hi.astype(F32)).astype(BF16)
        lhs = jnp.concatenate([km_hi, km_lo, k_ref[pl.ds(start, blk), hs]], axis=0)
        prods.append(_nt_dot(lhs, q_ref[:, hs]))
    probs = []
    for h, hs in enumerate(head_slices):
        g = jnp.where(past, prods[h][:nb] + prods[h][nb:2 * nb], NEG)
        rank = jnp.zeros((nb, blk), jnp.int32)
        for m in range(nb):
            gm = g[m:m + 1, :]
            beats = (gm > g) | ((gm == g) & (n_iota > m))
            rank = rank + beats.astype(jnp.int32)
        sel_sc[h] = ((rank < MOBA_TOPK) & past).astype(F32)
        s = prods[h][2 * nb:] + bias_ref[h, 0]
        m0 = jnp.max(s, axis=0, keepdims=True)
        m_sc[h] = m0
        probs.append(jnp.exp2(s - m0).astype(BF16))
    for h, hs in enumerate(head_slices):
        acc_sc[h] = jnp.dot(values(qb, hs), probs[h], preferred_element_type=F32)

    def past_blocks(n0, count, dists):
        st = pl.multiple_of(n0 * blk, blk)
        prods = [_nt_dot(k_ref[pl.ds(st, count * blk), hs], q_ref[:, hs]) for hs in head_slices]
        scaled = []
        for h, s_all in enumerate(prods):
            far_bias = rb_ref[h, REL_BUCKETS - 1] * LOG2E
            tiles, m_tile = [], None
            for j, dist in enumerate(dists):
                s = s_all[j * blk:(j + 1) * blk]
                if dist is None:
                    c = far_bias
                else:
                    s = s + bias_ref[h, dist]
                    c = 0.0
                chosen = sel_sc[h, pl.ds(n0 + j, 1), :] > 0.5
                m_j = jnp.where(chosen, jnp.max(s, axis=0, keepdims=True) + c, NEG)
                m_tile = m_j if m_tile is None else jnp.maximum(m_tile, m_j)
                tiles.append((s, chosen, c))
            m_run = m_sc[h]
            m_new = jnp.maximum(m_run, m_tile)
            m_sc[h] = m_new
            p = [jnp.exp2(s - jnp.where(chosen, m_new - c, -NEG)).astype(BF16) for s, chosen, c in tiles]
            scaled.append((jnp.exp2(m_run - m_new), jnp.concatenate(p, axis=0)))
        for h, hs in enumerate(head_slices):
            alpha, p = scaled[h]
            vals = jnp.concatenate([values(n0 + j, hs) for j in range(count)], axis=1)
            acc_sc[h] = alpha * acc_sc[h] + jnp.dot(vals, p, preferred_element_type=F32)

    def far_group(i, carry):
        past_blocks(i * FAR_GROUP, FAR_GROUP, (None,) * FAR_GROUP)
        return carry

    def far_single(n, carry):
        past_blocks(n, 1, (None,))
        return carry

    def near_single(n, carry):
        past_blocks(n, 1, (qb - n,))
        return carry

    near_count = NEAR_DIST - 1
    far_end = jnp.maximum(qb - near_count, 0)
    far_groups = far_end // FAR_GROUP
    lax.fori_loop(0, far_groups, far_group, 0)
    lax.fori_loop(far_groups * FAR_GROUP, far_end, far_single, 0)
    lax.fori_loop(0, jnp.where(qb < near_count, qb, 0), near_single, 0)

    @pl.when(qb >= near_count)
    def _():
        past_blocks(qb - near_count, near_count, tuple(range(near_count, 0, -1)))

    for h, hs in enumerate(head_slices):
        acc = acc_sc[h]
        o_ref[:, hs] = (acc[:HEAD_DIM] / acc[HEAD_DIM:HEAD_DIM + 1]).T.astype(BF16)


def _moba_attention(rel_bias, qkm, vt, bias_tab, *, batch, seq, heads):
    tokens = qkm.shape[0]
    nb = seq // MOBA_BLOCK
    width = heads * HEAD_DIM
    assert seq % MOBA_BLOCK == 0
    kern = functools.partial(_attn_kernel, nb=nb, heads=heads)
    return pl.pallas_call(
        kern,
        grid=(batch, nb),
        in_specs=[
            pl.BlockSpec(memory_space=pltpu.SMEM),
            pl.BlockSpec((MOBA_BLOCK, width), lambda b, qb: (b * nb + qb, 0)),
            pl.BlockSpec((seq, width), lambda b, qb: (b, 1)),
            pl.BlockSpec((nb, width, MOBA_BLOCK), lambda b, qb: (b, 0, 0)),
            pl.BlockSpec((heads, NEAR_DIST, MOBA_BLOCK, MOBA_BLOCK), lambda b, qb: (0, 0, 0, 0),
                         pipeline_mode=pl.Buffered(1)),
        ],
        out_specs=pl.BlockSpec((MOBA_BLOCK, width), lambda b, qb: (b * nb + qb, 0)),
        out_shape=jax.ShapeDtypeStruct((tokens, width), BF16),
        scratch_shapes=[pltpu.VMEM((nb, width), F32),
                        pltpu.VMEM((heads, nb, MOBA_BLOCK), F32),
                        pltpu.VMEM((heads, 1, MOBA_BLOCK), F32),
                        pltpu.VMEM((heads, HEAD_DIM + ONES_ROWS, MOBA_BLOCK), F32)],
        compiler_params=pltpu.CompilerParams(
            dimension_semantics=("arbitrary", "arbitrary"), vmem_limit_bytes=VMEM_LIMIT),
        name="moba_attn",
    )(rel_bias, qkm, qkm, vt, bias_tab)


def _outproj_kernel(x_ref, a_ref, m_ref, wa_ref, wm_ref, o_ref):
    o_ref[...] = (x_ref[...]
                  + jnp.dot(a_ref[...], wa_ref[...], preferred_element_type=F32)
                  + jnp.dot(m_ref[...], wm_ref[...], preferred_element_type=F32))


def _out_proj(x2, a, qkm, w_out, *, tm):
    tokens, d_model = x2.shape
    half = a.shape[1]
    m_col = qkm.shape[1] // half - 1
    return pl.pallas_call(
        _outproj_kernel,
        grid=(tokens // tm,),
        in_specs=[
            pl.BlockSpec((tm, d_model), lambda i: (i, 0)),
            pl.BlockSpec((tm, half), lambda i: (i, 0)),
            pl.BlockSpec((tm, half), lambda i: (i, m_col)),
            pl.BlockSpec((half, d_model), lambda i: (0, 0)),
            pl.BlockSpec((half, d_model), lambda i: (1, 0)),
        ],
        out_specs=pl.BlockSpec((tm, d_model), lambda i: (i, 0)),
        out_shape=jax.ShapeDtypeStruct((tokens, d_model), F32),
        compiler_params=pltpu.CompilerParams(
            dimension_semantics=("arbitrary",), vmem_limit_bytes=VMEM_LIMIT),
        name="out_proj",
    )(x2, a, qkm, w_out, w_out)


def _ffn_kernel(x_ref, g_ref, wug_ref, wuv_ref, cwg_ref, cwv_ref, cbg_ref, cbv_ref, wd_ref,
                o_ref, h_sc, u0_sc, u1_sc, carry_sc, *, tm, tf, d_model, tiles_per_seq, nf):
    i = pl.program_id(0)
    f = pl.program_id(1)
    first = (i % tiles_per_seq) == 0
    u_scs = (u0_sc, u1_sc)
    k_blocks = d_model // FFN_K_BLOCK
    chunk = tm // FFN_ROW_CHUNKS
    col_chunk = tf // FFN_COL_CHUNKS
    n_chunks = FFN_ROW_CHUNKS * FFN_COL_CHUNKS
    slices_per_chunk = 2 * k_blocks // n_chunks

    def up(slot, anchors):
        for half, w_ref in enumerate((wug_ref, wuv_ref)):
            acc = None
            for k in range(k_blocks):
                ks = slice(k * FFN_K_BLOCK, (k + 1) * FFN_K_BLOCK)
                lhs = h_sc[:, ks]
                n = half * k_blocks + k
                if anchors is not None and n % slices_per_chunk == 0 and n >= ANCHOR_LAG * slices_per_chunk:
                    top = pltpu.bitcast(lhs[:BF16_SUBLANES, :ANCHOR_LANES], jnp.uint32)
                    top = pltpu.bitcast(top | anchors[n // slices_per_chunk - ANCHOR_LAG], BF16)
                    top = jnp.concatenate([top, lhs[:BF16_SUBLANES, ANCHOR_LANES:]], axis=1)
                    lhs = jnp.concatenate([top, lhs[BF16_SUBLANES:, :]], axis=0)
                part = jnp.dot(lhs, w_ref[ks, :], preferred_element_type=F32)
                acc = part if acc is None else acc + part
            u_scs[slot][half] = acc

    def gate(slot):
        u_sc = u_scs[slot]
        rows, anchors = [], []
        for c in range(FFN_ROW_CHUNKS):
            r0 = c * chunk
            cols = []
            for cc in range(FFN_COL_CHUNKS):
                cs = slice(cc * col_chunk, (cc + 1) * col_chunk)
                ys = []
                for half, (cw_ref, cb_ref) in enumerate(((cwg_ref, cbg_ref), (cwv_ref, cbv_ref))):
                    if c == 0:
                        prev = jnp.where(first, 0.0, carry_sc[half, f - 1, :, cs])
                    else:
                        prev = u_sc[half, r0 - CONV_HALO:r0, cs]
                    cur = u_sc[half, r0:r0 + chunk, cs]
                    e = jnp.concatenate([prev, cur], axis=0)
                    y = cb_ref[:, cs]
                    for tap in range(CONV_WIDTH - 1):
                        y = y + cw_ref[tap:tap + 1, cs] * pltpu.roll(e, CONV_WIDTH - 1 - tap, axis=0)[CONV_HALO:, :]
                    ys.append(y + cw_ref[CONV_WIDTH - 1:CONV_WIDTH, cs] * cur)
                yg, yv = ys
                act = yg / (1.0 + jnp.exp2(yg * -LOG2E)) * yv
                cols.append(act.astype(BF16))
                words = pltpu.bitcast(act, jnp.uint32)
                tiles = [words[r:r + 8, l:l + ANCHOR_LANES] for r in range(0, chunk, 8)
                         for l in range(0, col_chunk, ANCHOR_LANES)]
                while len(tiles) > 1:
                    tiles = [a | b for a, b in zip(tiles[::2], tiles[1::2])]
                anchors.append((tiles[0] >> 16) >> 16)
            rows.append(jnp.concatenate(cols, axis=1))
        for half in range(2):
            carry_sc[half, f - 1] = u_sc[half, tm - CONV_HALO:, :]
        return jnp.concatenate(rows, axis=0), anchors

    def down(act):
        return jnp.dot(act, wd_ref[...], preferred_element_type=F32)

    @pl.when(f == 0)
    def _():
        h_sc[...] = _rms(x_ref[...], g_ref[...]).astype(BF16)
        up(0, None)

    @pl.when(f == 1)
    def _():
        act, anchors = gate(0)
        up(1, anchors)
        o_ref[...] = x_ref[...] + down(act)

    for parity in range(2):
        @pl.when((f >= 2) & (f < nf) & (f % 2 == parity))
        def _():
            act, anchors = gate(1 - parity)
            up(parity, anchors)
            o_ref[...] += down(act)

    @pl.when(f == nf)
    def _():
        act, _ = gate((nf - 1) % 2)
        o_ref[...] += down(act)


def _conv_ffn(x1, ffn_g, w_up, conv_w, conv_b, w_down, *, seq, tm, tf):
    tokens, d_model = x1.shape
    d_ff = w_down.shape[0]
    nf = d_ff // tf
    assert d_ff % tf == 0 and seq % tm == 0 and CONV_WIDTH - 1 <= CONV_HALO <= tm // FFN_ROW_CHUNKS
    assert d_model % FFN_K_BLOCK == 0 and (2 * d_model // FFN_K_BLOCK) % (FFN_ROW_CHUNKS * FFN_COL_CHUNKS) == 0 and nf >= 2
    kern = functools.partial(_ffn_kernel, tm=tm, tf=tf, d_model=d_model, tiles_per_seq=seq // tm, nf=nf)
    up_col = lambda f: jnp.minimum(f, nf - 1)
    down_col = lambda f: jnp.maximum(f - 1, 0)
    return pl.pallas_call(
        kern,
        grid=(tokens // tm, nf + 1),
        in_specs=[
            pl.BlockSpec((tm, d_model), lambda i, f: (i, 0)),
            pl.BlockSpec((1, d_model), lambda i, f: (0, 0)),
            pl.BlockSpec((d_model, tf), lambda i, f: (0, up_col(f))),
            pl.BlockSpec((d_model, tf), lambda i, f: (0, nf + up_col(f))),
            pl.BlockSpec((CONV_WIDTH, tf), lambda i, f: (0, down_col(f))),
            pl.BlockSpec((CONV_WIDTH, tf), lambda i, f: (0, nf + down_col(f))),
            pl.BlockSpec((1, tf), lambda i, f: (0, down_col(f))),
            pl.BlockSpec((1, tf), lambda i, f: (0, nf + down_col(f))),
            pl.BlockSpec((tf, d_model), lambda i, f: (down_col(f), 0)),
        ],
        out_specs=pl.BlockSpec((tm, d_model), lambda i, f: (i, 0)),
        out_shape=jax.ShapeDtypeStruct((tokens, d_model), F32),
        scratch_shapes=[pltpu.VMEM((tm, d_model), BF16),
                        pltpu.VMEM((2, tm, tf), F32),
                        pltpu.VMEM((2, tm, tf), F32),
                        pltpu.VMEM((2, nf, CONV_HALO, tf), F32)],
        compiler_params=pltpu.CompilerParams(
            dimension_semantics=("arbitrary", "arbitrary"), vmem_limit_bytes=VMEM_LIMIT),
        name="conv_ffn",
    )(x1, ffn_g, w_up, w_up, conv_w, conv_w, conv_b, conv_b, w_down)


def kernel(x, attn_norm_g, w_in, q_norm_g, k_norm_g, rel_bias, pool_w, pool_scale, w_out,
           ffn_norm_g, w_up, conv_w, conv_b, w_down):
    batch, seq, d_model = x.shape
    depth = w_in.shape[0]
    heads = rel_bias.shape[0]
    x2 = x.reshape(batch * seq, d_model)
    bias_tab = _bias_tables(rel_bias)
    for l in range(depth):
        qkm, vt, (w_out_l, w_up_l, w_down_l) = _in_proj(
            x2, attn_norm_g[l][None], w_in[l].astype(BF16), q_norm_g[l][None], k_norm_g[l][None],
            pool_w[l].astype(BF16), pool_scale[l][None], (w_out, w_up, w_down), l, seq=seq, tm=1024)
        a = _moba_attention(rel_bias, qkm, vt, bias_tab, batch=batch, seq=seq, heads=heads)
        x1 = _out_proj(x2, a, qkm, w_out_l, tm=512)
        x2 = _conv_ffn(x1, ffn_norm_g[l][None], w_up_l, conv_w[l], conv_b[l][None], w_down_l,
                       seq=seq, tm=512, tf=512)
    return x2.reshape(batch, seq, d_model)
```

```python
import functools
import math

import jax
import jax.numpy as jnp
from jax import lax
from jax.experimental import pallas as pl
from jax.experimental.pallas import tpu as pltpu

HEAD_DIM = 128
MOBA_BLOCK = 256
MOBA_TOPK = 3
POOL_WINDOWS = (2, 4, 8, 16)
REL_BUCKETS = 32
REL_MAX_DIST = 1024
CONV_WIDTH = 3
EPS = 1e-6
NEG = -1e30

POOL_HALO = 16
CONV_HALO = 8
NEAR_DIST = 5

ONES_ROWS = 16
LOG2E = math.log2(math.e)
FAR_GROUP = 2

VMEM_LIMIT = 56 * 1024 * 1024

BF16 = jnp.bfloat16
F32 = jnp.float32
BF16_SUBLANES = 16
MXU_DIM = 256
INPROJ_TIE_SLICE = 4
FFN_ROW_CHUNKS = 8
FFN_COL_CHUNKS = 2
ANCHOR_LANES = 128
ANCHOR_LAG = 2


def _rel_bucket_thresholds():
    max_exact = REL_BUCKETS // 2
    span = REL_BUCKETS - max_exact
    out = []
    for k in range(1, span):
        edge = max_exact * (REL_MAX_DIST / max_exact) ** (k / span)
        r = round(edge)
        out.append(r if abs(edge - r) < 1e-9 else math.ceil(edge))
    return tuple(out)


def _bias_kernel(rb_ref, o_ref):
    h = pl.program_id(0)
    blk = MOBA_BLOCK
    shape = (8, 2 * blk)
    m = lax.broadcasted_iota(jnp.int32, shape, 1)
    offset = jnp.where(m < blk, m, m - 2 * blk)
    max_exact = REL_BUCKETS // 2
    for d in range(NEAR_DIST):
        n = jnp.maximum(d * blk + offset, 0)
        large = jnp.full(shape, max_exact, jnp.int32)
        for t in _rel_bucket_thresholds():
            large = large + (n >= t).astype(jnp.int32)
        bucket = jnp.where(n < max_exact, n, large)
        val = jnp.zeros(shape, F32)
        for b in range(REL_BUCKETS):
            val = jnp.where(bucket == b, rb_ref[h, b], val)
        val = val * LOG2E
        if d == 0:
            val = jnp.where(offset < 0, NEG, val)
        strip = jnp.broadcast_to(val[0:1, :], (blk, 2 * blk))
        o_ref[0, d] = pltpu.roll(strip, 0, axis=1, stride=1, stride_axis=0)[:, :blk]


def _bias_tables(rel_bias):
    assert (NEAR_DIST - 1) * MOBA_BLOCK + 1 >= _rel_bucket_thresholds()[-1]
    heads = rel_bias.shape[0]
    return pl.pallas_call(
        _bias_kernel,
        grid=(heads,),
        in_specs=[pl.BlockSpec(memory_space=pltpu.SMEM)],
        out_specs=pl.BlockSpec((1, NEAR_DIST, MOBA_BLOCK, MOBA_BLOCK), lambda h: (h, 0, 0, 0)),
        out_shape=jax.ShapeDtypeStruct((heads, NEAR_DIST, MOBA_BLOCK, MOBA_BLOCK), F32),
        name="bias_tables",
    )(rel_bias)


def _rms(y, g):
    ms = jnp.mean(y * y, axis=-1, keepdims=True)
    return y * lax.rsqrt(ms + EPS) * g


def _zero_from(vals):
    words = pltpu.bitcast(vals, jnp.uint32)
    tiles = [words[r:r + 8, l:l + ANCHOR_LANES] for r in range(0, words.shape[0], 8)
             for l in range(0, words.shape[1], ANCHOR_LANES)]
    while len(tiles) > 1:
        tiles = [a | b for a, b in zip(tiles[::2], tiles[1::2])] + ([tiles[-1]] if len(tiles) % 2 else [])
    return (tiles[0] >> 16) >> 16


def _tie(lhs, zero):
    top = pltpu.bitcast(lhs[:BF16_SUBLANES, :ANCHOR_LANES], jnp.uint32)
    top = pltpu.bitcast(top | zero, BF16)
    top = jnp.concatenate([top, lhs[:BF16_SUBLANES, ANCHOR_LANES:]], axis=1)
    return jnp.concatenate([top, lhs[BF16_SUBLANES:, :]], axis=0)


def _inproj_kernel(x_ref, g_ref, w_ref, qg_ref, kg_ref, pw_ref, ps_ref, *rest,
                   tm, d_model, tiles_per_seq, group_width, n_side):
    side_in, (o_ref, vt_ref), rest = rest[:n_side], rest[n_side:n_side + 2], rest[n_side + 2:]
    side_out, (h_sc, halo_sc) = rest[:n_side], rest[n_side:]
    i = pl.program_id(0)
    j = pl.program_id(1)
    k_blocks = d_model // MXU_DIM
    width = group_width
    n_chunks = w_ref.shape[1] // width

    for src, dst in zip(side_in, side_out):
        dst[...] = src[...].astype(BF16)

    def project(n, zero, scale=None):
        cols = slice(n * width, (n + 1) * width)
        halves = [slice(0, tm // 2), slice(tm // 2, tm)]
        accs = [None, None]
        for k in range(k_blocks):
            ks = slice(k * MXU_DIM, (k + 1) * MXU_DIM)
            if scale is None:
                lhs = h_sc[:, ks]
            else:
                lhs = (x_ref[:, ks] * scale * g_ref[:, ks]).astype(BF16)
                h_sc[:, ks] = lhs
            if zero is not None and k == INPROJ_TIE_SLICE:
                lhs = _tie(lhs, zero)
            for r, rows in enumerate(halves):
                part = jnp.dot(lhs[rows, :], w_ref[ks, cols], preferred_element_type=F32)
                accs[r] = part if accs[r] is None else accs[r] + part
        return jnp.concatenate(accs, axis=0)

    def chunked(finish, scale=None, order=None, after=None):
        zero, pending = None, None
        for pos, n in enumerate(order or range(n_chunks)):
            y = project(n, zero, scale if pos == 0 else None)
            if pending is not None:
                after(*pending)
            zero, state = finish(n, y)
            pending = (n, state) if after is not None else None
        if pending is not None:
            after(*pending)

    def head_norm(g):
        def finish(n, y):
            outs = [_rms(y[:, c:c + HEAD_DIM], g).astype(BF16) for c in range(0, width, HEAD_DIM)]
            out = jnp.concatenate(outs, axis=1)
            o_ref[:, n * width:(n + 1) * width] = out
            return _zero_from(out), None
        return finish

    @pl.when(j == 0)
    def _():
        x = x_ref[...]
        scale = lax.rsqrt(jnp.mean(x * x, axis=-1, keepdims=True) + EPS)
        chunked(head_norm(qg_ref[...] * (HEAD_DIM ** -0.5 * LOG2E)), scale)

    @pl.when(j == 1)
    def _():
        chunked(head_norm(kg_ref[...]))

    @pl.when(j == 2)
    def _():
        def finish(n, y):
            outs = [y[r:r + MOBA_BLOCK, :].T.astype(BF16) for r in range(0, tm, MOBA_BLOCK)]
            for r, out in enumerate(outs):
                vt_ref[r, n * width:(n + 1) * width, :] = out
            return _zero_from(jnp.concatenate(outs, axis=0)), None
        chunked(finish)

    @pl.when(j == 3)
    def _():
        seq_tile = i % tiles_per_seq
        t = seq_tile * tm + lax.broadcasted_iota(jnp.int32, (tm, 1), 0)

        def finish(g, pg):
            sl = slice(g * width, (g + 1) * width)
            w = POOL_WINDOWS[g]
            halo = jnp.where(seq_tile == 0, 0.0, halo_sc[:, sl])
            halo_sc[:, sl] = pg[tm - POOL_HALO:, :]
            a = jnp.concatenate([halo, pg], axis=0)
            shift = 1
            while shift < w:
                a = a + pltpu.roll(a, shift, axis=0)
                shift *= 2
            count = jnp.minimum(t + 1, w).astype(F32)
            mixed = (a[POOL_HALO:, :] / count - pg).astype(BF16)
            return _zero_from(mixed), mixed

        def mix(g, mixed):
            sl = slice(g * width, (g + 1) * width)
            mg = jnp.dot(mixed, pw_ref[g], preferred_element_type=F32)
            o_ref[:, sl] = (mg * ps_ref[:, sl]).astype(BF16)

        chunked(finish, order=sorted(range(n_chunks), key=lambda g: -POOL_WINDOWS[g]), after=mix)


def _in_proj(x2, attn_g, w_in, q_g, k_g, pool_w, pool_scale, side, layer, *, seq, tm):
    tokens, d_model = x2.shape
    in_width = w_in.shape[1]
    tn = in_width // 4
    heads = tn // HEAD_DIM
    groups, group_width, _ = pool_w.shape
    assert groups == len(POOL_WINDOWS) and groups * group_width == tn
    assert seq % tm == 0 and tm >= POOL_HALO and tm % MOBA_BLOCK == 0
    assert group_width % HEAD_DIM == 0 and d_model % MXU_DIM == 0 and INPROJ_TIE_SLICE < d_model // MXU_DIM
    kern = functools.partial(_inproj_kernel, tm=tm, d_model=d_model, tiles_per_seq=seq // tm,
                             group_width=group_width, n_side=len(side))
    out_col = lambda i, j: (i, jnp.minimum(j, 1) + j // 3)
    steps = (tokens // tm) * 4
    slab_in, slab_out = [], []
    for a in side:
        rows = a.shape[1]
        hold = next(h for h in (1, 2, 4, 8) if steps % h == 0 and rows % (steps // h * BF16_SUBLANES) == 0)
        slab = (rows // (steps // hold), a.shape[2])
        slab_in.append(pl.BlockSpec((None,) + slab, lambda i, j, hold=hold: (layer, (i * 4 + j) // hold, 0)))
        slab_out.append(pl.BlockSpec(slab, lambda i, j, hold=hold: ((i * 4 + j) // hold, 0)))
    outs = pl.pallas_call(
        kern,
        grid=(tokens // tm, 4),
        in_specs=[
            pl.BlockSpec((tm, d_model), lambda i, j: (i, 0)),
            pl.BlockSpec((1, d_model), lambda i, j: (0, 0)),
            pl.BlockSpec((d_model, tn), lambda i, j: (0, j)),
            pl.BlockSpec((1, HEAD_DIM), lambda i, j: (0, 0)),
            pl.BlockSpec((1, HEAD_DIM), lambda i, j: (0, 0)),
            pl.BlockSpec((groups, group_width, group_width), lambda i, j: (0, 0, 0)),
            pl.BlockSpec((1, tn), lambda i, j: (0, 0)),
        ] + slab_in,
        out_specs=[pl.BlockSpec((tm, tn), out_col),
                   pl.BlockSpec((tm // MOBA_BLOCK, tn, MOBA_BLOCK), lambda i, j: (i, 0, 0))] + slab_out,
        out_shape=[jax.ShapeDtypeStruct((tokens, 3 * tn), BF16),
                   jax.ShapeDtypeStruct((tokens // MOBA_BLOCK, tn, MOBA_BLOCK), BF16)]
        + [jax.ShapeDtypeStruct(a.shape[1:], BF16) for a in side],
        scratch_shapes=[pltpu.VMEM((tm, d_model), BF16), pltpu.VMEM((POOL_HALO, tn), F32)],
        compiler_params=pltpu.CompilerParams(
            dimension_semantics=("arbitrary", "arbitrary"), vmem_limit_bytes=VMEM_LIMIT),
        name="in_proj",
    )(x2, attn_g, w_in, q_g, k_g, pool_w, pool_scale, *side)
    return outs[0], outs[1], outs[2:]


def _nt_dot(a, b):
    return lax.dot_general(a, b, (((1,), (1,)), ((), ())), preferred_element_type=F32)


def _attn_kernel(rb_ref, q_ref, k_ref, vt_ref, bias_ref, o_ref, kmean_sc, sel_sc, m_sc, acc_sc,
                 *, nb, heads):
    qb = pl.program_id(1)
    blk = MOBA_BLOCK

    @pl.when(qb == 0)
    def _():
        for n in range(nb):
            kn = k_ref[n * blk:(n + 1) * blk, :].astype(F32)
            kmean_sc[n:n + 1, :] = jnp.mean(kn, axis=0, keepdims=True)

    n_iota = lax.broadcasted_iota(jnp.int32, (nb, blk), 0)
    past = n_iota < qb
    start = pl.multiple_of(qb * blk, blk)
    head_slices = [slice(h * HEAD_DIM, (h + 1) * HEAD_DIM) for h in range(heads)]
    ones = jnp.ones((ONES_ROWS, blk), BF16)

    def values(n, hs):
        return jnp.concatenate([vt_ref[n, hs, :], ones], axis=0)


    prods = []
    for hs in head_slices:
        km = kmean_sc[:, hs]
        km_hi = km.astype(BF16)
        km_lo = (km - km_hi.astype(F32)).astype(BF16)
        lhs = jnp.concatenate([km_hi, km_lo, k_ref[pl.ds(start, blk), hs]], axis=0)
        prods.append(_nt_dot(lhs, q_ref[:, hs]))
    probs = []
    for h, hs in enumerate(head_slices):
        g = jnp.where(past, prods[h][:nb] + prods[h][nb:2 * nb], NEG)
        rank = jnp.zeros((nb, blk), jnp.int32)
        for m in range(nb):
            gm = g[m:m + 1, :]
            beats = (gm > g) | ((gm == g) & (n_iota > m))
            rank = rank + beats.astype(jnp.int32)
        sel_sc[h] = ((rank < MOBA_TOPK) & past).astype(F32)
        s = prods[h][2 * nb:] + bias_ref[h, 0]
        m0 = jnp.max(s, axis=0, keepdims=True)
        m_sc[h] = m0
        probs.append(jnp.exp2(s - m0).astype(BF16))
    for h, hs in enumerate(head_slices):
        acc_sc[h] = jnp.dot(values(qb, hs), probs[h], preferred_element_type=F32)

    def past_blocks(n0, count, dists):
        st = pl.multiple_of(n0 * blk, blk)
        prods = [_nt_dot(k_ref[pl.ds(st, count * blk), hs], q_ref[:, hs]) for hs in head_slices]
        scaled = []
        for h, s_all in enumerate(prods):
            far_bias = rb_ref[h, REL_BUCKETS - 1] * LOG2E
            tiles, m_tile = [], None
            for j, dist in enumerate(dists):
                s = s_all[j * blk:(j + 1) * blk]
                if dist is None:
                    c = far_bias
                else:
                    s = s + bias_ref[h, dist]
                    c = 0.0
                chosen = sel_sc[h, pl.ds(n0 + j, 1), :] > 0.5
                m_j = jnp.where(chosen, jnp.max(s, axis=0, keepdims=True) + c, NEG)
                m_tile = m_j if m_tile is None else jnp.maximum(m_tile, m_j)
                tiles.append((s, chosen, c))
            m_run = m_sc[h]
            m_new = jnp.maximum(m_run, m_tile)
            m_sc[h] = m_new
            p = [jnp.exp2(s - jnp.where(chosen, m_new - c, -NEG)).astype(BF16) for s, chosen, c in tiles]
            scaled.append((jnp.exp2(m_run - m_new), jnp.concatenate(p, axis=0)))
        for h, hs in enumerate(head_slices):
            alpha, p = scaled[h]
            vals = jnp.concatenate([values(n0 + j, hs) for j in range(count)], axis=1)
            acc_sc[h] = alpha * acc_sc[h] + jnp.dot(vals, p, preferred_element_type=F32)

    def far_group(i, carry):
        past_blocks(i * FAR_GROUP, FAR_GROUP, (None,) * FAR_GROUP)
        return carry

    def far_single(n, carry):
        past_blocks(n, 1, (None,))
        return carry

    def near_single(n, carry):
        past_blocks(n, 1, (qb - n,))
        return carry

    near_count = NEAR_DIST - 1
    far_end = jnp.maximum(qb - near_count, 0)
    far_groups = far_end // FAR_GROUP
    lax.fori_loop(0, far_groups, far_group, 0)
    lax.fori_loop(far_groups * FAR_GROUP, far_end, far_single, 0)
    lax.fori_loop(0, jnp.where(qb < near_count, qb, 0), near_single, 0)

    @pl.when(qb >= near_count)
    def _():
        past_blocks(qb - near_count, near_count, tuple(range(near_count, 0, -1)))

    for h, hs in enumerate(head_slices):
        acc = acc_sc[h]
        o_ref[:, hs] = (acc[:HEAD_DIM] / acc[HEAD_DIM:HEAD_DIM + 1]).T.astype(BF16)


def _moba_attention(rel_bias, qkm, vt, bias_tab, *, batch, seq, heads):
    tokens = qkm.shape[0]
    nb = seq // MOBA_BLOCK
    width = heads * HEAD_DIM
    assert seq % MOBA_BLOCK == 0
    kern = functools.partial(_attn_kernel, nb=nb, heads=heads)
    return pl.pallas_call(
        kern,
        grid=(batch, nb),
        in_specs=[
            pl.BlockSpec(memory_space=pltpu.SMEM),
            pl.BlockSpec((MOBA_BLOCK, width), lambda b, qb: (b * nb + qb, 0)),
            pl.BlockSpec((seq, width), lambda b, qb: (b, 1)),
            pl.BlockSpec((nb, width, MOBA_BLOCK), lambda b, qb: (b, 0, 0)),
            pl.BlockSpec((heads, NEAR_DIST, MOBA_BLOCK, MOBA_BLOCK), lambda b, qb: (0, 0, 0, 0),
                         pipeline_mode=pl.Buffered(1)),
        ],
        out_specs=pl.BlockSpec((MOBA_BLOCK, width), lambda b, qb: (b * nb + qb, 0)),
        out_shape=jax.ShapeDtypeStruct((tokens, width), BF16),
        scratch_shapes=[pltpu.VMEM((nb, width), F32),
                        pltpu.VMEM((heads, nb, MOBA_BLOCK), F32),
                        pltpu.VMEM((heads, 1, MOBA_BLOCK), F32),
                        pltpu.VMEM((heads, HEAD_DIM + ONES_ROWS, MOBA_BLOCK), F32)],
        compiler_params=pltpu.CompilerParams(
            dimension_semantics=("arbitrary", "arbitrary"), vmem_limit_bytes=VMEM_LIMIT),
        name="moba_attn",
    )(rel_bias, qkm, qkm, vt, bias_tab)


def _outproj_kernel(x_ref, a_ref, m_ref, wa_ref, wm_ref, o_ref):
    o_ref[...] = (x_ref[...]
                  + jnp.dot(a_ref[...], wa_ref[...], preferred_element_type=F32)
                  + jnp.dot(m_ref[...], wm_ref[...], preferred_element_type=F32))


def _out_proj(x2, a, qkm, w_out, *, tm):
    tokens, d_model = x2.shape
    half = a.shape[1]
    m_col = qkm.shape[1] // half - 1
    return pl.pallas_call(
        _outproj_kernel,
        grid=(tokens // tm,),
        in_specs=[
            pl.BlockSpec((tm, d_model), lambda i: (i, 0)),
            pl.BlockSpec((tm, half), lambda i: (i, 0)),
            pl.BlockSpec((tm, half), lambda i: (i, m_col)),
            pl.BlockSpec((half, d_model), lambda i: (0, 0)),
            pl.BlockSpec((half, d_model), lambda i: (1, 0)),
        ],
        out_specs=pl.BlockSpec((tm, d_model), lambda i: (i, 0)),
        out_shape=jax.ShapeDtypeStruct((tokens, d_model), F32),
        compiler_params=pltpu.CompilerParams(
            dimension_semantics=("arbitrary",), vmem_limit_bytes=VMEM_LIMIT),
        name="out_proj",
    )(x2, a, qkm, w_out, w_out)


def _ffn_kernel(x_ref, g_ref, wug_ref, wuv_ref, cwg_ref, cwv_ref, cbg_ref, cbv_ref, wd_ref,
                o_ref, h_sc, u0_sc, u1_sc, carry_sc, *, tm, tf, d_model, tiles_per_seq, nf):
    i = pl.program_id(0)
    f = pl.program_id(1)
    first = (i % tiles_per_seq) == 0
    u_scs = (u0_sc, u1_sc)
    k_blocks = d_model // MXU_DIM
    chunk = tm // FFN_ROW_CHUNKS
    col_chunk = tf // FFN_COL_CHUNKS
    n_chunks = FFN_ROW_CHUNKS * FFN_COL_CHUNKS
    slices_per_chunk = 2 * k_blocks // n_chunks

    def up(slot, anchors):
        for half, w_ref in enumerate((wug_ref, wuv_ref)):
            acc = None
            for k in range(k_blocks):
                ks = slice(k * MXU_DIM, (k + 1) * MXU_DIM)
                lhs = h_sc[:, ks]
                n = half * k_blocks + k
                if anchors is not None and n % slices_per_chunk == 0 and n >= ANCHOR_LAG * slices_per_chunk:
                    lhs = _tie(lhs, anchors[n // slices_per_chunk - ANCHOR_LAG])
                part = jnp.dot(lhs, w_ref[ks, :], preferred_element_type=F32)
                acc = part if acc is None else acc + part
            u_scs[slot][half] = acc

    def gate(slot):
        u_sc = u_scs[slot]
        rows, anchors = [], []
        for c in range(FFN_ROW_CHUNKS):
            r0 = c * chunk
            cols = []
            for cc in range(FFN_COL_CHUNKS):
                cs = slice(cc * col_chunk, (cc + 1) * col_chunk)
                ys = []
                for half, (cw_ref, cb_ref) in enumerate(((cwg_ref, cbg_ref), (cwv_ref, cbv_ref))):
                    if c == 0:
                        prev = jnp.where(first, 0.0, carry_sc[half, f - 1, :, cs])
                    else:
                        prev = u_sc[half, r0 - CONV_HALO:r0, cs]
                    cur = u_sc[half, r0:r0 + chunk, cs]
                    e = jnp.concatenate([prev, cur], axis=0)
                    y = cb_ref[:, cs]
                    for tap in range(CONV_WIDTH - 1):
                        y = y + cw_ref[tap:tap + 1, cs] * pltpu.roll(e, CONV_WIDTH - 1 - tap, axis=0)[CONV_HALO:, :]
                    ys.append(y + cw_ref[CONV_WIDTH - 1:CONV_WIDTH, cs] * cur)
                yg, yv = ys
                act = yg / (1.0 + jnp.exp2(yg * -LOG2E)) * yv
                cols.append(act.astype(BF16))
                anchors.append(_zero_from(act))
            rows.append(jnp.concatenate(cols, axis=1))
        for half in range(2):
            carry_sc[half, f - 1] = u_sc[half, tm - CONV_HALO:, :]
        return jnp.concatenate(rows, axis=0), anchors

    def down(act):
        return jnp.dot(act, wd_ref[...], preferred_element_type=F32)

    @pl.when(f == 0)
    def _():
        h_sc[...] = _rms(x_ref[...], g_ref[...]).astype(BF16)
        up(0, None)

    @pl.when(f == 1)
    def _():
        act, anchors = gate(0)
        up(1, anchors)
        o_ref[...] = x_ref[...] + down(act)

    for parity in range(2):
        @pl.when((f >= 2) & (f < nf) & (f % 2 == parity))
        def _():
            act, anchors = gate(1 - parity)
            up(parity, anchors)
            o_ref[...] += down(act)

    @pl.when(f == nf)
    def _():
        act, _ = gate((nf - 1) % 2)
        o_ref[...] += down(act)


def _conv_ffn(x1, ffn_g, w_up, conv_w, conv_b, w_down, *, seq, tm, tf):
    tokens, d_model = x1.shape
    d_ff = w_down.shape[0]
    nf = d_ff // tf
    assert d_ff % tf == 0 and seq % tm == 0 and CONV_WIDTH - 1 <= CONV_HALO <= tm // FFN_ROW_CHUNKS
    assert d_model % MXU_DIM == 0 and (2 * d_model // MXU_DIM) % (FFN_ROW_CHUNKS * FFN_COL_CHUNKS) == 0 and nf >= 2
    kern = functools.partial(_ffn_kernel, tm=tm, tf=tf, d_model=d_model, tiles_per_seq=seq // tm, nf=nf)
    up_col = lambda f: jnp.minimum(f, nf - 1)
    down_col = lambda f: jnp.maximum(f - 1, 0)
    return pl.pallas_call(
        kern,
        grid=(tokens // tm, nf + 1),
        in_specs=[
            pl.BlockSpec((tm, d_model), lambda i, f: (i, 0)),
            pl.BlockSpec((1, d_model), lambda i, f: (0, 0)),
            pl.BlockSpec((d_model, tf), lambda i, f: (0, up_col(f))),
            pl.BlockSpec((d_model, tf), lambda i, f: (0, nf + up_col(f))),
            pl.BlockSpec((CONV_WIDTH, tf), lambda i, f: (0, down_col(f))),
            pl.BlockSpec((CONV_WIDTH, tf), lambda i, f: (0, nf + down_col(f))),
            pl.BlockSpec((1, tf), lambda i, f: (0, down_col(f))),
            pl.BlockSpec((1, tf), lambda i, f: (0, nf + down_col(f))),
            pl.BlockSpec((tf, d_model), lambda i, f: (down_col(f), 0)),
        ],
        out_specs=pl.BlockSpec((tm, d_model), lambda i, f: (i, 0)),
        out_shape=jax.ShapeDtypeStruct((tokens, d_model), F32),
        scratch_shapes=[pltpu.VMEM((tm, d_model), BF16),
                        pltpu.VMEM((2, tm, tf), F32),
                        pltpu.VMEM((2, tm, tf), F32),
                        pltpu.VMEM((2, nf, CONV_HALO, tf), F32)],
        compiler_params=pltpu.CompilerParams(
            dimension_semantics=("arbitrary", "arbitrary"), vmem_limit_bytes=VMEM_LIMIT),
        name="conv_ffn",
    )(x1, ffn_g, w_up, w_up, conv_w, conv_w, conv_b, conv_b, w_down)


def kernel(x, attn_norm_g, w_in, q_norm_g, k_norm_g, rel_bias, pool_w, pool_scale, w_out,
           ffn_norm_g, w_up, conv_w, conv_b, w_down):
    batch, seq, d_model = x.shape
    depth = w_in.shape[0]
    heads = rel_bias.shape[0]
    x2 = x.reshape(batch * seq, d_model)
    bias_tab = _bias_tables(rel_bias)
    for l in range(depth):
        qkm, vt, (w_out_l, w_up_l, w_down_l) = _in_proj(
            x2, attn_norm_g[l][None], w_in[l].astype(BF16), q_norm_g[l][None], k_norm_g[l][None],
            pool_w[l].astype(BF16), pool_scale[l][None], (w_out, w_up, w_down), l, seq=seq, tm=1024)
        a = _moba_attention(rel_bias, qkm, vt, bias_tab, batch=batch, seq=seq, heads=heads)
        x1 = _out_proj(x2, a, qkm, w_out_l, tm=512)
        x2 = _conv_ffn(x1, ffn_norm_g[l][None], w_up_l, conv_w[l], conv_b[l][None], w_down_l,
                       seq=seq, tm=512, tf=512)
    return x2.reshape(batch, seq, d_model)
```

```python
import functools
import math

import jax
import jax.numpy as jnp
from jax import lax
from jax.experimental import pallas as pl
from jax.experimental.pallas import tpu as pltpu

HEAD_DIM = 128
MOBA_BLOCK = 256
MOBA_TOPK = 3
POOL_WINDOWS = (2, 4, 8, 16)
REL_BUCKETS = 32
REL_MAX_DIST = 1024
CONV_WIDTH = 3
EPS = 1e-6
NEG = -1e30

POOL_HALO = 16
CONV_HALO = 8
NEAR_DIST = 5

ONES_ROWS = 16
LOG2E = math.log2(math.e)
FAR_GROUP = 2

VMEM_LIMIT = 56 * 1024 * 1024

BF16 = jnp.bfloat16
F32 = jnp.float32
BF16_SUBLANES = 16
MXU_DIM = 256
INPROJ_TIE_SLICE = 4
FFN_ROW_CHUNKS = 8
FFN_COL_CHUNKS = 2
ANCHOR_LANES = 128
ANCHOR_LAG = 2


def _rel_bucket_thresholds():
    max_exact = REL_BUCKETS // 2
    span = REL_BUCKETS - max_exact
    out = []
    for k in range(1, span):
        edge = max_exact * (REL_MAX_DIST / max_exact) ** (k / span)
        r = round(edge)
        out.append(r if abs(edge - r) < 1e-9 else math.ceil(edge))
    return tuple(out)


def _bias_kernel(rb_ref, o_ref):
    h = pl.program_id(0)
    blk = MOBA_BLOCK
    shape = (8, 2 * blk)
    m = lax.broadcasted_iota(jnp.int32, shape, 1)
    offset = jnp.where(m < blk, m, m - 2 * blk)
    max_exact = REL_BUCKETS // 2
    for d in range(NEAR_DIST):
        n = jnp.maximum(d * blk + offset, 0)
        large = jnp.full(shape, max_exact, jnp.int32)
        for t in _rel_bucket_thresholds():
            large = large + (n >= t).astype(jnp.int32)
        bucket = jnp.where(n < max_exact, n, large)
        val = jnp.zeros(shape, F32)
        for b in range(REL_BUCKETS):
            val = jnp.where(bucket == b, rb_ref[h, b], val)
        val = val * LOG2E
        if d == 0:
            val = jnp.where(offset < 0, NEG, val)
        strip = jnp.broadcast_to(val[0:1, :], (blk, 2 * blk))
        o_ref[0, d] = pltpu.roll(strip, 0, axis=1, stride=1, stride_axis=0)[:, :blk]


def _bias_tables(rel_bias):
    assert (NEAR_DIST - 1) * MOBA_BLOCK + 1 >= _rel_bucket_thresholds()[-1]
    heads = rel_bias.shape[0]
    return pl.pallas_call(
        _bias_kernel,
        grid=(heads,),
        in_specs=[pl.BlockSpec(memory_space=pltpu.SMEM)],
        out_specs=pl.BlockSpec((1, NEAR_DIST, MOBA_BLOCK, MOBA_BLOCK), lambda h: (h, 0, 0, 0)),
        out_shape=jax.ShapeDtypeStruct((heads, NEAR_DIST, MOBA_BLOCK, MOBA_BLOCK), F32),
        name="bias_tables",
    )(rel_bias)


def _rms(y, g):
    ms = jnp.mean(y * y, axis=-1, keepdims=True)
    return y * lax.rsqrt(ms + EPS) * g


def _zero_from(vals):
    words = pltpu.bitcast(vals, jnp.uint32)
    tiles = [words[r:r + 8, l:l + ANCHOR_LANES] for r in range(0, words.shape[0], 8)
             for l in range(0, words.shape[1], ANCHOR_LANES)]
    while len(tiles) > 1:
        tiles = [a | b for a, b in zip(tiles[::2], tiles[1::2])] + ([tiles[-1]] if len(tiles) % 2 else [])
    return (tiles[0] >> 16) >> 16


def _tie(lhs, zero):
    top = pltpu.bitcast(lhs[:BF16_SUBLANES, :ANCHOR_LANES], jnp.uint32)
    top = pltpu.bitcast(top | zero, BF16)
    top = jnp.concatenate([top, lhs[:BF16_SUBLANES, ANCHOR_LANES:]], axis=1)
    return jnp.concatenate([top, lhs[BF16_SUBLANES:, :]], axis=0)


def _inproj_kernel(x_ref, g_ref, w_ref, qg_ref, kg_ref, pw_ref, ps_ref, *rest,
                   tm, d_model, tiles_per_seq, group_width, n_side):
    side_in, (o_ref, vt_ref), rest = rest[:n_side], rest[n_side:n_side + 2], rest[n_side + 2:]
    side_out, (h_sc, halo_sc) = rest[:n_side], rest[n_side:]
    i = pl.program_id(0)
    j = pl.program_id(1)
    k_blocks = d_model // MXU_DIM
    width = group_width
    n_chunks = w_ref.shape[1] // width

    for src, dst in zip(side_in, side_out):
        dst[...] = src[...].astype(BF16)

    def project(n, zero, scale=None):
        cols = slice(n * width, (n + 1) * width)
        halves = [slice(0, tm // 2), slice(tm // 2, tm)]
        accs = [None, None]
        for k in range(k_blocks):
            ks = slice(k * MXU_DIM, (k + 1) * MXU_DIM)
            if scale is None:
                lhs = h_sc[:, ks]
            else:
                lhs = (x_ref[:, ks] * scale * g_ref[:, ks]).astype(BF16)
                h_sc[:, ks] = lhs
            if zero is not None and k == INPROJ_TIE_SLICE:
                lhs = _tie(lhs, zero)
            for r, rows in enumerate(halves):
                part = jnp.dot(lhs[rows, :], w_ref[ks, cols], preferred_element_type=F32)
                accs[r] = part if accs[r] is None else accs[r] + part
        return jnp.concatenate(accs, axis=0)

    def chunked(finish, scale=None, order=None, after=None):
        zero, pending = None, None
        for pos, n in enumerate(order or range(n_chunks)):
            y = project(n, zero, scale if pos == 0 else None)
            if pending is not None:
                after(*pending)
            zero, state = finish(n, y)
            pending = (n, state) if after is not None else None
        if pending is not None:
            after(*pending)

    def head_norm(g):
        def finish(n, y):
            outs = [_rms(y[:, c:c + HEAD_DIM], g).astype(BF16) for c in range(0, width, HEAD_DIM)]
            out = jnp.concatenate(outs, axis=1)
            o_ref[:, n * width:(n + 1) * width] = out
            return _zero_from(out), None
        return finish

    @pl.when(j == 0)
    def _():
        x = x_ref[...]
        scale = lax.rsqrt(jnp.mean(x * x, axis=-1, keepdims=True) + EPS)
        chunked(head_norm(qg_ref[...] * (HEAD_DIM ** -0.5 * LOG2E)), scale)

    @pl.when(j == 1)
    def _():
        chunked(head_norm(kg_ref[...]))

    @pl.when(j == 2)
    def _():
        def finish(n, y):
            outs = [y[r:r + MOBA_BLOCK, :].T.astype(BF16) for r in range(0, tm, MOBA_BLOCK)]
            for r, out in enumerate(outs):
                vt_ref[r, n * width:(n + 1) * width, :] = out
            return _zero_from(jnp.concatenate(outs, axis=0)), None
        chunked(finish)

    @pl.when(j == 3)
    def _():
        seq_tile = i % tiles_per_seq
        t_top = seq_tile * tm + lax.broadcasted_iota(jnp.int32, (POOL_HALO, 1), 0)

        def finish(g, pg):
            sl = slice(g * width, (g + 1) * width)
            w = POOL_WINDOWS[g]
            halo = jnp.where(seq_tile == 0, 0.0, halo_sc[:, sl])
            halo_sc[:, sl] = pg[tm - POOL_HALO:, :]
            a = jnp.concatenate([halo, pg], axis=0)
            shift = 1
            while shift < w:
                a = a + pltpu.roll(a, shift, axis=0)
                shift *= 2
            sums = a[POOL_HALO:, :]
            count_top = jnp.minimum(t_top + 1, w).astype(F32)
            mean = jnp.concatenate([sums[:POOL_HALO, :] / count_top, sums[POOL_HALO:, :] / w], axis=0)
            mixed = (mean - pg).astype(BF16)
            return _zero_from(mixed), mixed

        def mix(g, mixed):
            sl = slice(g * width, (g + 1) * width)
            mg = jnp.dot(mixed, pw_ref[g], preferred_element_type=F32)
            o_ref[:, sl] = (mg * ps_ref[:, sl]).astype(BF16)

        chunked(finish, order=sorted(range(n_chunks), key=lambda g: -POOL_WINDOWS[g]), after=mix)


def _in_proj(x2, attn_g, w_in, q_g, k_g, pool_w, pool_scale, side, layer, *, seq, tm):
    tokens, d_model = x2.shape
    in_width = w_in.shape[1]
    tn = in_width // 4
    groups, group_width, _ = pool_w.shape
    assert groups == len(POOL_WINDOWS) and groups * group_width == tn
    assert seq % tm == 0 and tm >= 2 * POOL_HALO >= 2 * (max(POOL_WINDOWS) - 1) and tm % MOBA_BLOCK == 0
    assert group_width % HEAD_DIM == 0 and d_model % MXU_DIM == 0 and INPROJ_TIE_SLICE < d_model // MXU_DIM
    kern = functools.partial(_inproj_kernel, tm=tm, d_model=d_model, tiles_per_seq=seq // tm,
                             group_width=group_width, n_side=len(side))
    out_col = lambda i, j: (i, jnp.minimum(j, 1) + j // 3)
    steps = (tokens // tm) * 4
    slab_in, slab_out = [], []
    for a in side:
        rows = a.shape[1]
        hold = next(h for h in (1, 2, 4, 8) if steps % h == 0 and rows % (steps // h * BF16_SUBLANES) == 0)
        slab = (rows // (steps // hold), a.shape[2])
        slab_in.append(pl.BlockSpec((None,) + slab, lambda i, j, hold=hold: (layer, (i * 4 + j) // hold, 0)))
        slab_out.append(pl.BlockSpec(slab, lambda i, j, hold=hold: ((i * 4 + j) // hold, 0)))
    outs = pl.pallas_call(
        kern,
        grid=(tokens // tm, 4),
        in_specs=[
            pl.BlockSpec((tm, d_model), lambda i, j: (i, 0)),
            pl.BlockSpec((1, d_model), lambda i, j: (0, 0)),
            pl.BlockSpec((d_model, tn), lambda i, j: (0, j)),
            pl.BlockSpec((1, HEAD_DIM), lambda i, j: (0, 0)),
            pl.BlockSpec((1, HEAD_DIM), lambda i, j: (0, 0)),
            pl.BlockSpec((groups, group_width, group_width), lambda i, j: (0, 0, 0)),
            pl.BlockSpec((1, tn), lambda i, j: (0, 0)),
        ] + slab_in,
        out_specs=[pl.BlockSpec((tm, tn), out_col),
                   pl.BlockSpec((tm // MOBA_BLOCK, tn, MOBA_BLOCK), lambda i, j: (i, 0, 0))] + slab_out,
        out_shape=[jax.ShapeDtypeStruct((tokens, 3 * tn), BF16),
                   jax.ShapeDtypeStruct((tokens // MOBA_BLOCK, tn, MOBA_BLOCK), BF16)]
        + [jax.ShapeDtypeStruct(a.shape[1:], BF16) for a in side],
        scratch_shapes=[pltpu.VMEM((tm, d_model), BF16), pltpu.VMEM((POOL_HALO, tn), F32)],
        compiler_params=pltpu.CompilerParams(
            dimension_semantics=("arbitrary", "arbitrary"), vmem_limit_bytes=VMEM_LIMIT),
        name="in_proj",
    )(x2, attn_g, w_in, q_g, k_g, pool_w, pool_scale, *side)
    return outs[0], outs[1], outs[2:]


def _nt_dot(a, b):
    return lax.dot_general(a, b, (((1,), (1,)), ((), ())), preferred_element_type=F32)


def _attn_kernel(rb_ref, q_ref, k_ref, vt_ref, bias_ref, o_ref, kmean_sc, sel_sc, m_sc, acc_sc,
                 *, nb, heads):
    qb = pl.program_id(1)
    blk = MOBA_BLOCK

    @pl.when(qb == 0)
    def _():
        for n in range(nb):
            kn = k_ref[n * blk:(n + 1) * blk, :].astype(F32)
            kmean_sc[n:n + 1, :] = jnp.mean(kn, axis=0, keepdims=True)

    n_iota = lax.broadcasted_iota(jnp.int32, (nb, blk), 0)
    past = n_iota < qb
    start = pl.multiple_of(qb * blk, blk)
    head_slices = [slice(h * HEAD_DIM, (h + 1) * HEAD_DIM) for h in range(heads)]
    ones = jnp.ones((ONES_ROWS, blk), BF16)

    def values(n, hs):
        return jnp.concatenate([vt_ref[n, hs, :], ones], axis=0)


    prods = []
    for hs in head_slices:
        km = kmean_sc[:, hs]
        km_hi = km.astype(BF16)
        km_lo = (km - km_hi.astype(F32)).astype(BF16)
        lhs = jnp.concatenate([km_hi, km_lo, k_ref[pl.ds(start, blk), hs]], axis=0)
        prods.append(_nt_dot(lhs, q_ref[:, hs]))
    probs = []
    for h, hs in enumerate(head_slices):
        g = jnp.where(past, prods[h][:nb] + prods[h][nb:2 * nb], NEG)
        rank = jnp.zeros((nb, blk), jnp.int32)
        for m in range(nb):
            gm = g[m:m + 1, :]
            beats = (gm > g) | ((gm == g) & (n_iota > m))
            rank = rank + beats.astype(jnp.int32)
        sel_sc[h] = ((rank < MOBA_TOPK) & past).astype(F32)
        s = prods[h][2 * nb:] + bias_ref[h, 0]
        m0 = jnp.max(s, axis=0, keepdims=True)
        m_sc[h] = m0
        probs.append(jnp.exp2(s - m0).astype(BF16))
    for h, hs in enumerate(head_slices):
        acc_sc[h] = jnp.dot(values(qb, hs), probs[h], preferred_element_type=F32)

    def past_blocks(n0, count, dists):
        st = pl.multiple_of(n0 * blk, blk)
        prods = [_nt_dot(k_ref[pl.ds(st, count * blk), hs], q_ref[:, hs]) for hs in head_slices]
        scaled = []
        for h, s_all in enumerate(prods):
            far_bias = rb_ref[h, REL_BUCKETS - 1] * LOG2E
            tiles, m_tile = [], None
            for j, dist in enumerate(dists):
                s = s_all[j * blk:(j + 1) * blk]
                if dist is None:
                    c = far_bias
                else:
                    s = s + bias_ref[h, dist]
                    c = 0.0
                chosen = sel_sc[h, pl.ds(n0 + j, 1), :] > 0.5
                m_j = jnp.where(chosen, jnp.max(s, axis=0, keepdims=True) + c, NEG)
                m_tile = m_j if m_tile is None else jnp.maximum(m_tile, m_j)
                tiles.append((s, chosen, c))
            m_run = m_sc[h]
            m_new = jnp.maximum(m_run, m_tile)
            m_sc[h] = m_new
            p = [jnp.exp2(s - jnp.where(chosen, m_new - c, -NEG)).astype(BF16) for s, chosen, c in tiles]
            scaled.append((jnp.exp2(m_run - m_new), jnp.concatenate(p, axis=0)))
        for h, hs in enumerate(head_slices):
            alpha, p = scaled[h]
            vals = jnp.concatenate([values(n0 + j, hs) for j in range(count)], axis=1)
            acc_sc[h] = alpha * acc_sc[h] + jnp.dot(vals, p, preferred_element_type=F32)

    def far_group(i, carry):
        past_blocks(i * FAR_GROUP, FAR_GROUP, (None,) * FAR_GROUP)
        return carry

    def far_single(n, carry):
        past_blocks(n, 1, (None,))
        return carry

    def near_single(n, carry):
        past_blocks(n, 1, (qb - n,))
        return carry

    near_count = NEAR_DIST - 1
    far_end = jnp.maximum(qb - near_count, 0)
    far_groups = far_end // FAR_GROUP
    lax.fori_loop(0, far_groups, far_group, 0)
    lax.fori_loop(far_groups * FAR_GROUP, far_end, far_single, 0)
    lax.fori_loop(0, jnp.where(qb < near_count, qb, 0), near_single, 0)

    @pl.when(qb >= near_count)
    def _():
        past_blocks(qb - near_count, near_count, tuple(range(near_count, 0, -1)))

    for h, hs in enumerate(head_slices):
        acc = acc_sc[h]
        o_ref[:, hs] = (acc[:HEAD_DIM] / acc[HEAD_DIM:HEAD_DIM + 1]).T.astype(BF16)


def _moba_attention(rel_bias, qkm, vt, bias_tab, *, batch, seq, heads):
    tokens = qkm.shape[0]
    nb = seq // MOBA_BLOCK
    width = heads * HEAD_DIM
    assert seq % MOBA_BLOCK == 0
    kern = functools.partial(_attn_kernel, nb=nb, heads=heads)
    return pl.pallas_call(
        kern,
        grid=(batch, nb),
        in_specs=[
            pl.BlockSpec(memory_space=pltpu.SMEM),
            pl.BlockSpec((MOBA_BLOCK, width), lambda b, qb: (b * nb + qb, 0)),
            pl.BlockSpec((seq, width), lambda b, qb: (b, 1)),
            pl.BlockSpec((nb, width, MOBA_BLOCK), lambda b, qb: (b, 0, 0)),
            pl.BlockSpec((heads, NEAR_DIST, MOBA_BLOCK, MOBA_BLOCK), lambda b, qb: (0, 0, 0, 0),
                         pipeline_mode=pl.Buffered(1)),
        ],
        out_specs=pl.BlockSpec((MOBA_BLOCK, width), lambda b, qb: (b * nb + qb, 0)),
        out_shape=jax.ShapeDtypeStruct((tokens, width), BF16),
        scratch_shapes=[pltpu.VMEM((nb, width), F32),
                        pltpu.VMEM((heads, nb, MOBA_BLOCK), F32),
                        pltpu.VMEM((heads, 1, MOBA_BLOCK), F32),
                        pltpu.VMEM((heads, HEAD_DIM + ONES_ROWS, MOBA_BLOCK), F32)],
        compiler_params=pltpu.CompilerParams(
            dimension_semantics=("arbitrary", "arbitrary"), vmem_limit_bytes=VMEM_LIMIT),
        name="moba_attn",
    )(rel_bias, qkm, qkm, vt, bias_tab)


def _outproj_kernel(x_ref, a_ref, m_ref, wa_ref, wm_ref, o_ref):
    o_ref[...] = (x_ref[...]
                  + jnp.dot(a_ref[...], wa_ref[...], preferred_element_type=F32)
                  + jnp.dot(m_ref[...], wm_ref[...], preferred_element_type=F32))


def _out_proj(x2, a, qkm, w_out, *, tm):
    tokens, d_model = x2.shape
    half = a.shape[1]
    m_col = qkm.shape[1] // half - 1
    return pl.pallas_call(
        _outproj_kernel,
        grid=(tokens // tm,),
        in_specs=[
            pl.BlockSpec((tm, d_model), lambda i: (i, 0)),
            pl.BlockSpec((tm, half), lambda i: (i, 0)),
            pl.BlockSpec((tm, half), lambda i: (i, m_col)),
            pl.BlockSpec((half, d_model), lambda i: (0, 0)),
            pl.BlockSpec((half, d_model), lambda i: (1, 0)),
        ],
        out_specs=pl.BlockSpec((tm, d_model), lambda i: (i, 0)),
        out_shape=jax.ShapeDtypeStruct((tokens, d_model), F32),
        compiler_params=pltpu.CompilerParams(
            dimension_semantics=("arbitrary",), vmem_limit_bytes=VMEM_LIMIT),
        name="out_proj",
    )(x2, a, qkm, w_out, w_out)


def _ffn_kernel(x_ref, g_ref, wug_ref, wuv_ref, cwg_ref, cwv_ref, cbg_ref, cbv_ref, wd_ref,
                o_ref, h_sc, u0_sc, u1_sc, carry_sc, *, tm, tf, d_model, tiles_per_seq, nf):
    i = pl.program_id(0)
    f = pl.program_id(1)
    first = (i % tiles_per_seq) == 0
    u_scs = (u0_sc, u1_sc)
    k_blocks = d_model // MXU_DIM
    chunk = tm // FFN_ROW_CHUNKS
    col_chunk = tf // FFN_COL_CHUNKS
    n_chunks = FFN_ROW_CHUNKS * FFN_COL_CHUNKS
    slices_per_chunk = 2 * k_blocks // n_chunks

    def up(slot, anchors):
        for half, w_ref in enumerate((wug_ref, wuv_ref)):
            acc = None
            for k in range(k_blocks):
                ks = slice(k * MXU_DIM, (k + 1) * MXU_DIM)
                lhs = h_sc[:, ks]
                n = half * k_blocks + k
                if anchors is not None and n % slices_per_chunk == 0 and n >= ANCHOR_LAG * slices_per_chunk:
                    lhs = _tie(lhs, anchors[n // slices_per_chunk - ANCHOR_LAG])
                part = jnp.dot(lhs, w_ref[ks, :], preferred_element_type=F32)
                acc = part if acc is None else acc + part
            u_scs[slot][half] = acc

    def gate(slot):
        u_sc = u_scs[slot]
        rows, anchors = [], []
        for c in range(FFN_ROW_CHUNKS):
            r0 = c * chunk
            cols = []
            for cc in range(FFN_COL_CHUNKS):
                cs = slice(cc * col_chunk, (cc + 1) * col_chunk)
                ys = []
                for half, (cw_ref, cb_ref) in enumerate(((cwg_ref, cbg_ref), (cwv_ref, cbv_ref))):
                    if c == 0:
                        prev = jnp.where(first, 0.0, carry_sc[half, f - 1, :, cs])
                    else:
                        prev = u_sc[half, r0 - CONV_HALO:r0, cs]
                    cur = u_sc[half, r0:r0 + chunk, cs]
                    e = jnp.concatenate([prev, cur], axis=0)
                    y = cb_ref[:, cs]
                    for tap in range(CONV_WIDTH - 1):
                        y = y + cw_ref[tap:tap + 1, cs] * pltpu.roll(e, CONV_WIDTH - 1 - tap, axis=0)[CONV_HALO:, :]
                    ys.append(y + cw_ref[CONV_WIDTH - 1:CONV_WIDTH, cs] * cur)
                yg, yv = ys
                act = yg / (1.0 + jnp.exp2(yg * -LOG2E)) * yv
                cols.append(act.astype(BF16))
                anchors.append(_zero_from(act))
            rows.append(jnp.concatenate(cols, axis=1))
        for half in range(2):
            carry_sc[half, f - 1] = u_sc[half, tm - CONV_HALO:, :]
        return jnp.concatenate(rows, axis=0), anchors

    def down(act):
        return jnp.dot(act, wd_ref[...], preferred_element_type=F32)

    @pl.when(f == 0)
    def _():
        h_sc[...] = _rms(x_ref[...], g_ref[...]).astype(BF16)
        up(0, None)

    @pl.when(f == 1)
    def _():
        act, anchors = gate(0)
        up(1, anchors)
        o_ref[...] = x_ref[...] + down(act)

    for parity in range(2):
        @pl.when((f >= 2) & (f < nf) & (f % 2 == parity))
        def _():
            act, anchors = gate(1 - parity)
            up(parity, anchors)
            o_ref[...] += down(act)

    @pl.when(f == nf)
    def _():
        act, _ = gate((nf - 1) % 2)
        o_ref[...] += down(act)


def _conv_ffn(x1, ffn_g, w_up, conv_w, conv_b, w_down, *, seq, tm, tf):
    tokens, d_model = x1.shape
    d_ff = w_down.shape[0]
    nf = d_ff // tf
    assert d_ff % tf == 0 and seq % tm == 0 and CONV_WIDTH - 1 <= CONV_HALO <= tm // FFN_ROW_CHUNKS
    assert d_model % MXU_DIM == 0 and (2 * d_model // MXU_DIM) % (FFN_ROW_CHUNKS * FFN_COL_CHUNKS) == 0 and nf >= 2
    kern = functools.partial(_ffn_kernel, tm=tm, tf=tf, d_model=d_model, tiles_per_seq=seq // tm, nf=nf)
    up_col = lambda f: jnp.minimum(f, nf - 1)
    down_col = lambda f: jnp.maximum(f - 1, 0)
    return pl.pallas_call(
        kern,
        grid=(tokens // tm, nf + 1),
        in_specs=[
            pl.BlockSpec((tm, d_model), lambda i, f: (i, 0)),
            pl.BlockSpec((1, d_model), lambda i, f: (0, 0)),
            pl.BlockSpec((d_model, tf), lambda i, f: (0, up_col(f))),
            pl.BlockSpec((d_model, tf), lambda i, f: (0, nf + up_col(f))),
            pl.BlockSpec((CONV_WIDTH, tf), lambda i, f: (0, down_col(f))),
            pl.BlockSpec((CONV_WIDTH, tf), lambda i, f: (0, nf + down_col(f))),
            pl.BlockSpec((1, tf), lambda i, f: (0, down_col(f))),
            pl.BlockSpec((1, tf), lambda i, f: (0, nf + down_col(f))),
            pl.BlockSpec((tf, d_model), lambda i, f: (down_col(f), 0)),
        ],
        out_specs=pl.BlockSpec((tm, d_model), lambda i, f: (i, 0)),
        out_shape=jax.ShapeDtypeStruct((tokens, d_model), F32),
        scratch_shapes=[pltpu.VMEM((tm, d_model), BF16),
                        pltpu.VMEM((2, tm, tf), F32),
                        pltpu.VMEM((2, tm, tf), F32),
                        pltpu.VMEM((2, nf, CONV_HALO, tf), F32)],
        compiler_params=pltpu.CompilerParams(
            dimension_semantics=("arbitrary", "arbitrary"), vmem_limit_bytes=VMEM_LIMIT),
        name="conv_ffn",
    )(x1, ffn_g, w_up, w_up, conv_w, conv_w, conv_b, conv_b, w_down)


def kernel(x, attn_norm_g, w_in, q_norm_g, k_norm_g, rel_bias, pool_w, pool_scale, w_out,
           ffn_norm_g, w_up, conv_w, conv_b, w_down):
    batch, seq, d_model = x.shape
    depth = w_in.shape[0]
    heads = rel_bias.shape[0]
    x2 = x.reshape(batch * seq, d_model)
    bias_tab = _bias_tables(rel_bias)
    for l in range(depth):
        qkm, vt, (w_out_l, w_up_l, w_down_l) = _in_proj(
            x2, attn_norm_g[l][None], w_in[l].astype(BF16), q_norm_g[l][None], k_norm_g[l][None],
            pool_w[l].astype(BF16), pool_scale[l][None], (w_out, w_up, w_down), l, seq=seq, tm=1024)
        a = _moba_attention(rel_bias, qkm, vt, bias_tab, batch=batch, seq=seq, heads=heads)
        x1 = _out_proj(x2, a, qkm, w_out_l, tm=512)
        x2 = _conv_ffn(x1, ffn_norm_g[l][None], w_up_l, conv_w[l], conv_b[l][None], w_down_l,
                       seq=seq, tm=512, tf=512)
    return x2.reshape(batch, seq, d_model)
```

```python
import functools
import math

import jax
import jax.numpy as jnp
from jax import lax
from jax.experimental import pallas as pl
from jax.experimental.pallas import tpu as pltpu

HEAD_DIM = 128
MOBA_BLOCK = 256
MOBA_TOPK = 3
POOL_WINDOWS = (2, 4, 8, 16)
REL_BUCKETS = 32
REL_MAX_DIST = 1024
CONV_WIDTH = 3
EPS = 1e-6
NEG = -1e30

POOL_HALO = 16
CONV_HALO = 8
NEAR_DIST = 5

ONES_ROWS = 16
LOG2E = math.log2(math.e)
FAR_GROUP = 2

VMEM_LIMIT = 56 * 1024 * 1024

BF16 = jnp.bfloat16
F32 = jnp.float32
BF16_SUBLANES = 16
MXU_DIM = 256
INPROJ_TIE_SLICE = 4
FFN_ROW_CHUNKS = 8
FFN_COL_CHUNKS = 2
ANCHOR_LANES = 128
ANCHOR_LAG = 2


def _rel_bucket_thresholds():
    max_exact = REL_BUCKETS // 2
    span = REL_BUCKETS - max_exact
    out = []
    for k in range(1, span):
        edge = max_exact * (REL_MAX_DIST / max_exact) ** (k / span)
        r = round(edge)
        out.append(r if abs(edge - r) < 1e-9 else math.ceil(edge))
    return tuple(out)


def _bias_kernel(rb_ref, w_ref, o_ref, wb_ref):
    wb_ref[...] = w_ref[...].astype(BF16)

    h = pl.program_id(0)
    blk = MOBA_BLOCK
    shape = (8, 2 * blk)
    m = lax.broadcasted_iota(jnp.int32, shape, 1)
    offset = jnp.where(m < blk, m, m - 2 * blk)
    max_exact = REL_BUCKETS // 2
    for d in range(NEAR_DIST):
        n = jnp.maximum(d * blk + offset, 0)
        large = jnp.full(shape, max_exact, jnp.int32)
        for t in _rel_bucket_thresholds():
            large = large + (n >= t).astype(jnp.int32)
        bucket = jnp.where(n < max_exact, n, large)
        val = jnp.zeros(shape, F32)
        for b in range(REL_BUCKETS):
            val = jnp.where(bucket == b, rb_ref[h, b], val)
        val = val * LOG2E
        if d == 0:
            val = jnp.where(offset < 0, NEG, val)
        strip = jnp.broadcast_to(val[0:1, :], (blk, 2 * blk))
        o_ref[0, d] = pltpu.roll(strip, 0, axis=1, stride=1, stride_axis=0)[:, :blk]


def _bias_tables(rel_bias, w_in):
    assert (NEAR_DIST - 1) * MOBA_BLOCK + 1 >= _rel_bucket_thresholds()[-1]
    heads = rel_bias.shape[0]
    depth, rows, cols = w_in.shape
    assert rows % (heads * BF16_SUBLANES) == 0
    slab = pl.BlockSpec((depth, rows // heads, cols), lambda h: (0, h, 0))
    return pl.pallas_call(
        _bias_kernel,
        grid=(heads,),
        in_specs=[pl.BlockSpec(memory_space=pltpu.SMEM), slab],
        out_specs=[pl.BlockSpec((1, NEAR_DIST, MOBA_BLOCK, MOBA_BLOCK), lambda h: (h, 0, 0, 0)), slab],
        out_shape=[jax.ShapeDtypeStruct((heads, NEAR_DIST, MOBA_BLOCK, MOBA_BLOCK), F32),
                   jax.ShapeDtypeStruct(w_in.shape, BF16)],
        compiler_params=pltpu.CompilerParams(vmem_limit_bytes=VMEM_LIMIT),
        name="bias_tables",
    )(rel_bias, w_in)


def _rms(y, g):
    ms = jnp.mean(y * y, axis=-1, keepdims=True)
    return y * lax.rsqrt(ms + EPS) * g


def _zero_from(vals):
    words = pltpu.bitcast(vals, jnp.uint32)
    tiles = [words[r:r + 8, l:l + ANCHOR_LANES] for r in range(0, words.shape[0], 8)
             for l in range(0, words.shape[1], ANCHOR_LANES)]
    while len(tiles) > 1:
        tiles = [a | b for a, b in zip(tiles[::2], tiles[1::2])] + ([tiles[-1]] if len(tiles) % 2 else [])
    return (tiles[0] >> 16) >> 16


def _tie(lhs, zero):
    top = pltpu.bitcast(lhs[:BF16_SUBLANES, :ANCHOR_LANES], jnp.uint32)
    top = pltpu.bitcast(top | zero, BF16)
    top = jnp.concatenate([top, lhs[:BF16_SUBLANES, ANCHOR_LANES:]], axis=1)
    return jnp.concatenate([top, lhs[BF16_SUBLANES:, :]], axis=0)


def _inproj_kernel(x_ref, g_ref, w_ref, qg_ref, kg_ref, pw_ref, ps_ref, *rest,
                   tm, d_model, tiles_per_seq, group_width, n_side):
    side_in, (o_ref, vt_ref), rest = rest[:n_side], rest[n_side:n_side + 2], rest[n_side + 2:]
    side_out, (h_sc, halo_sc) = rest[:n_side], rest[n_side:]
    i = pl.program_id(0)
    j = pl.program_id(1)
    k_blocks = d_model // MXU_DIM
    width = group_width
    n_chunks = w_ref.shape[1] // width

    for src, dst in zip(side_in, side_out):
        dst[...] = src[...].astype(BF16)

    def project(n, zero, scale=None):
        cols = slice(n * width, (n + 1) * width)
        halves = [slice(0, tm // 2), slice(tm // 2, tm)]
        accs = [None, None]
        for k in range(k_blocks):
            ks = slice(k * MXU_DIM, (k + 1) * MXU_DIM)
            if scale is None:
                lhs = h_sc[:, ks]
            else:
                lhs = (x_ref[:, ks] * scale * g_ref[:, ks]).astype(BF16)
                h_sc[:, ks] = lhs
            if zero is not None and k == INPROJ_TIE_SLICE:
                lhs = _tie(lhs, zero)
            for r, rows in enumerate(halves):
                part = jnp.dot(lhs[rows, :], w_ref[ks, cols], preferred_element_type=F32)
                accs[r] = part if accs[r] is None else accs[r] + part
        return jnp.concatenate(accs, axis=0)

    def chunked(finish, scale=None, order=None, after=None):
        zero, pending = None, None
        for pos, n in enumerate(order or range(n_chunks)):
            y = project(n, zero, scale if pos == 0 else None)
            if pending is not None:
                after(*pending)
            zero, state = finish(n, y)
            pending = (n, state) if after is not None else None
        if pending is not None:
            after(*pending)

    def head_norm(g):
        def finish(n, y):
            outs = [_rms(y[:, c:c + HEAD_DIM], g).astype(BF16) for c in range(0, width, HEAD_DIM)]
            out = jnp.concatenate(outs, axis=1)
            o_ref[:, n * width:(n + 1) * width] = out
            return _zero_from(out), None
        return finish

    @pl.when(j == 0)
    def _():
        x = x_ref[...]
        scale = lax.rsqrt(jnp.mean(x * x, axis=-1, keepdims=True) + EPS)
        chunked(head_norm(qg_ref[...] * (HEAD_DIM ** -0.5 * LOG2E)), scale)

    @pl.when(j == 1)
    def _():
        chunked(head_norm(kg_ref[...]))

    @pl.when(j == 2)
    def _():
        def finish(n, y):
            outs = [y[r:r + MOBA_BLOCK, :].T.astype(BF16) for r in range(0, tm, MOBA_BLOCK)]
            for r, out in enumerate(outs):
                vt_ref[r, n * width:(n + 1) * width, :] = out
            return _zero_from(jnp.concatenate(outs, axis=0)), None
        chunked(finish)

    @pl.when(j == 3)
    def _():
        seq_tile = i % tiles_per_seq
        t_top = seq_tile * tm + lax.broadcasted_iota(jnp.int32, (POOL_HALO, 1), 0)

        def finish(g, pg):
            sl = slice(g * width, (g + 1) * width)
            w = POOL_WINDOWS[g]
            halo = jnp.where(seq_tile == 0, 0.0, halo_sc[:, sl])
            halo_sc[:, sl] = pg[tm - POOL_HALO:, :]
            a = jnp.concatenate([halo, pg], axis=0)
            shift = 1
            while shift < w:
                a = a + pltpu.roll(a, shift, axis=0)
                shift *= 2
            sums = a[POOL_HALO:, :]
            count_top = jnp.minimum(t_top + 1, w).astype(F32)
            mean = jnp.concatenate([sums[:POOL_HALO, :] / count_top, sums[POOL_HALO:, :] / w], axis=0)
            mixed = (mean - pg).astype(BF16)
            return _zero_from(mixed), mixed

        def mix(g, mixed):
            sl = slice(g * width, (g + 1) * width)
            mg = jnp.dot(mixed, pw_ref[g], preferred_element_type=F32)
            o_ref[:, sl] = (mg * ps_ref[:, sl]).astype(BF16)

        chunked(finish, order=sorted(range(n_chunks), key=lambda g: -POOL_WINDOWS[g]), after=mix)


def _in_proj(x2, attn_g, w_in, q_g, k_g, pool_w, pool_scale, side, layer, *, seq, tm):
    tokens, d_model = x2.shape
    in_width = w_in.shape[1]
    tn = in_width // 4
    groups, group_width, _ = pool_w.shape
    assert groups == len(POOL_WINDOWS) and groups * group_width == tn
    assert seq % tm == 0 and tm >= 2 * POOL_HALO >= 2 * (max(POOL_WINDOWS) - 1) and tm % MOBA_BLOCK == 0
    assert group_width % HEAD_DIM == 0 and d_model % MXU_DIM == 0 and INPROJ_TIE_SLICE < d_model // MXU_DIM
    kern = functools.partial(_inproj_kernel, tm=tm, d_model=d_model, tiles_per_seq=seq // tm,
                             group_width=group_width, n_side=len(side))
    out_col = lambda i, j: (i, jnp.minimum(j, 1) + j // 3)
    steps = (tokens // tm) * 4
    slab_in, slab_out = [], []
    for a in side:
        rows = a.shape[1]
        hold = next(h for h in (1, 2, 4, 8) if steps % h == 0 and rows % (steps // h * BF16_SUBLANES) == 0)
        slab = (rows // (steps // hold), a.shape[2])
        slab_in.append(pl.BlockSpec((None,) + slab, lambda i, j, hold=hold: (layer, (i * 4 + j) // hold, 0)))
        slab_out.append(pl.BlockSpec(slab, lambda i, j, hold=hold: ((i * 4 + j) // hold, 0)))
    outs = pl.pallas_call(
        kern,
        grid=(tokens // tm, 4),
        in_specs=[
            pl.BlockSpec((tm, d_model), lambda i, j: (i, 0)),
            pl.BlockSpec((1, d_model), lambda i, j: (0, 0)),
            pl.BlockSpec((d_model, tn), lambda i, j: (0, j)),
            pl.BlockSpec((1, HEAD_DIM), lambda i, j: (0, 0)),
            pl.BlockSpec((1, HEAD_DIM), lambda i, j: (0, 0)),
            pl.BlockSpec((groups, group_width, group_width), lambda i, j: (0, 0, 0)),
            pl.BlockSpec((1, tn), lambda i, j: (0, 0)),
        ] + slab_in,
        out_specs=[pl.BlockSpec((tm, tn), out_col),
                   pl.BlockSpec((tm // MOBA_BLOCK, tn, MOBA_BLOCK), lambda i, j: (i, 0, 0))] + slab_out,
        out_shape=[jax.ShapeDtypeStruct((tokens, 3 * tn), BF16),
                   jax.ShapeDtypeStruct((tokens // MOBA_BLOCK, tn, MOBA_BLOCK), BF16)]
        + [jax.ShapeDtypeStruct(a.shape[1:], BF16) for a in side],
        scratch_shapes=[pltpu.VMEM((tm, d_model), BF16), pltpu.VMEM((POOL_HALO, tn), F32)],
        compiler_params=pltpu.CompilerParams(
            dimension_semantics=("arbitrary", "arbitrary"), vmem_limit_bytes=VMEM_LIMIT),
        name="in_proj",
    )(x2, attn_g, w_in, q_g, k_g, pool_w, pool_scale, *side)
    return outs[0], outs[1], outs[2:]


def _nt_dot(a, b):
    return lax.dot_general(a, b, (((1,), (1,)), ((), ())), preferred_element_type=F32)


def _attn_kernel(rb_ref, q_ref, k_ref, vt_ref, bias_ref, o_ref, kmean_sc, sel_sc, m_sc, acc_sc,
                 *, nb, heads):
    qb = pl.program_id(1)
    blk = MOBA_BLOCK

    @pl.when(qb == 0)
    def _():
        for n in range(nb):
            kn = k_ref[n * blk:(n + 1) * blk, :].astype(F32)
            kmean_sc[n:n + 1, :] = jnp.mean(kn, axis=0, keepdims=True)

    n_iota = lax.broadcasted_iota(jnp.int32, (nb, blk), 0)
    past = n_iota < qb
    start = pl.multiple_of(qb * blk, blk)
    head_slices = [slice(h * HEAD_DIM, (h + 1) * HEAD_DIM) for h in range(heads)]
    ones = jnp.ones((ONES_ROWS, blk), BF16)

    def values(n, hs):
        return jnp.concatenate([vt_ref[n, hs, :], ones], axis=0)


    prods = []
    for hs in head_slices:
        km = kmean_sc[:, hs]
        km_hi = km.astype(BF16)
        km_lo = (km - km_hi.astype(F32)).astype(BF16)
        lhs = jnp.concatenate([km_hi, km_lo, k_ref[pl.ds(start, blk), hs]], axis=0)
        prods.append(_nt_dot(lhs, q_ref[:, hs]))
    probs = []
    for h, hs in enumerate(head_slices):
        g = jnp.where(past, prods[h][:nb] + prods[h][nb:2 * nb], NEG)
        rank = jnp.zeros((nb, blk), jnp.int32)
        for m in range(nb):
            gm = g[m:m + 1, :]
            beats = (gm > g) | ((gm == g) & (n_iota > m))
            rank = rank + beats.astype(jnp.int32)
        sel_sc[h] = ((rank < MOBA_TOPK) & past).astype(F32)
        s = prods[h][2 * nb:] + bias_ref[h, 0]
        m0 = jnp.max(s, axis=0, keepdims=True)
        m_sc[h] = m0
        probs.append(jnp.exp2(s - m0).astype(BF16))
    for h, hs in enumerate(head_slices):
        acc_sc[h] = jnp.dot(values(qb, hs), probs[h], preferred_element_type=F32)

    def past_blocks(n0, count, dists):
        st = pl.multiple_of(n0 * blk, blk)
        prods = [_nt_dot(k_ref[pl.ds(st, count * blk), hs], q_ref[:, hs]) for hs in head_slices]
        scaled = []
        for h, s_all in enumerate(prods):
            far_bias = rb_ref[h, REL_BUCKETS - 1] * LOG2E
            tiles, m_tile = [], None
            for j, dist in enumerate(dists):
                s = s_all[j * blk:(j + 1) * blk]
                if dist is None:
                    c = far_bias
                else:
                    s = s + bias_ref[h, dist]
                    c = 0.0
                chosen = sel_sc[h, pl.ds(n0 + j, 1), :] > 0.5
                m_j = jnp.where(chosen, jnp.max(s, axis=0, keepdims=True) + c, NEG)
                m_tile = m_j if m_tile is None else jnp.maximum(m_tile, m_j)
                tiles.append((s, chosen, c))
            m_run = m_sc[h]
            m_new = jnp.maximum(m_run, m_tile)
            m_sc[h] = m_new
            p = [jnp.exp2(s - jnp.where(chosen, m_new - c, -NEG)).astype(BF16) for s, chosen, c in tiles]
            scaled.append((jnp.exp2(m_run - m_new), jnp.concatenate(p, axis=0)))
        for h, hs in enumerate(head_slices):
            alpha, p = scaled[h]
            vals = jnp.concatenate([values(n0 + j, hs) for j in range(count)], axis=1)
            acc_sc[h] = alpha * acc_sc[h] + jnp.dot(vals, p, preferred_element_type=F32)

    def far_group(i, carry):
        past_blocks(i * FAR_GROUP, FAR_GROUP, (None,) * FAR_GROUP)
        return carry

    def far_single(n, carry):
        past_blocks(n, 1, (None,))
        return carry

    def near_single(n, carry):
        past_blocks(n, 1, (qb - n,))
        return carry

    near_count = NEAR_DIST - 1
    far_end = jnp.maximum(qb - near_count, 0)
    far_groups = far_end // FAR_GROUP
    lax.fori_loop(0, far_groups, far_group, 0)
    lax.fori_loop(far_groups * FAR_GROUP, far_end, far_single, 0)
    lax.fori_loop(0, jnp.where(qb < near_count, qb, 0), near_single, 0)

    @pl.when(qb >= near_count)
    def _():
        past_blocks(qb - near_count, near_count, tuple(range(near_count, 0, -1)))

    for h, hs in enumerate(head_slices):
        acc = acc_sc[h]
        o_ref[:, hs] = (acc[:HEAD_DIM] / acc[HEAD_DIM:HEAD_DIM + 1]).T.astype(BF16)


def _moba_attention(rel_bias, qkm, vt, bias_tab, *, batch, seq, heads):
    tokens = qkm.shape[0]
    nb = seq // MOBA_BLOCK
    width = heads * HEAD_DIM
    assert seq % MOBA_BLOCK == 0
    kern = functools.partial(_attn_kernel, nb=nb, heads=heads)
    return pl.pallas_call(
        kern,
        grid=(batch, nb),
        in_specs=[
            pl.BlockSpec(memory_space=pltpu.SMEM),
            pl.BlockSpec((MOBA_BLOCK, width), lambda b, qb: (b * nb + qb, 0)),
            pl.BlockSpec((seq, width), lambda b, qb: (b, 1)),
            pl.BlockSpec((nb, width, MOBA_BLOCK), lambda b, qb: (b, 0, 0)),
            pl.BlockSpec((heads, NEAR_DIST, MOBA_BLOCK, MOBA_BLOCK), lambda b, qb: (0, 0, 0, 0),
                         pipeline_mode=pl.Buffered(1)),
        ],
        out_specs=pl.BlockSpec((MOBA_BLOCK, width), lambda b, qb: (b * nb + qb, 0)),
        out_shape=jax.ShapeDtypeStruct((tokens, width), BF16),
        scratch_shapes=[pltpu.VMEM((nb, width), F32),
                        pltpu.VMEM((heads, nb, MOBA_BLOCK), F32),
                        pltpu.VMEM((heads, 1, MOBA_BLOCK), F32),
                        pltpu.VMEM((heads, HEAD_DIM + ONES_ROWS, MOBA_BLOCK), F32)],
        compiler_params=pltpu.CompilerParams(
            dimension_semantics=("arbitrary", "arbitrary"), vmem_limit_bytes=VMEM_LIMIT),
        name="moba_attn",
    )(rel_bias, qkm, qkm, vt, bias_tab)


def _outproj_kernel(x_ref, a_ref, m_ref, wa_ref, wm_ref, o_ref):
    o_ref[...] = (x_ref[...]
                  + jnp.dot(a_ref[...], wa_ref[...], preferred_element_type=F32)
                  + jnp.dot(m_ref[...], wm_ref[...], preferred_element_type=F32))


def _out_proj(x2, a, qkm, w_out, *, tm):
    tokens, d_model = x2.shape
    half = a.shape[1]
    m_col = qkm.shape[1] // half - 1
    return pl.pallas_call(
        _outproj_kernel,
        grid=(tokens // tm,),
        in_specs=[
            pl.BlockSpec((tm, d_model), lambda i: (i, 0)),
            pl.BlockSpec((tm, half), lambda i: (i, 0)),
            pl.BlockSpec((tm, half), lambda i: (i, m_col)),
            pl.BlockSpec((half, d_model), lambda i: (0, 0)),
            pl.BlockSpec((half, d_model), lambda i: (1, 0)),
        ],
        out_specs=pl.BlockSpec((tm, d_model), lambda i: (i, 0)),
        out_shape=jax.ShapeDtypeStruct((tokens, d_model), F32),
        compiler_params=pltpu.CompilerParams(
            dimension_semantics=("arbitrary",), vmem_limit_bytes=VMEM_LIMIT),
        name="out_proj",
    )(x2, a, qkm, w_out, w_out)


def _ffn_kernel(x_ref, g_ref, wug_ref, wuv_ref, cwg_ref, cwv_ref, cbg_ref, cbv_ref, wd_ref,
                o_ref, h_sc, u0_sc, u1_sc, carry_sc, *, tm, tf, d_model, tiles_per_seq, nf):
    i = pl.program_id(0)
    f = pl.program_id(1)
    first = (i % tiles_per_seq) == 0
    u_scs = (u0_sc, u1_sc)
    k_blocks = d_model // MXU_DIM
    chunk = tm // FFN_ROW_CHUNKS
    col_chunk = tf // FFN_COL_CHUNKS
    n_chunks = FFN_ROW_CHUNKS * FFN_COL_CHUNKS
    slices_per_chunk = 2 * k_blocks // n_chunks

    def up(slot, anchors):
        for half, w_ref in enumerate((wug_ref, wuv_ref)):
            acc = None
            for k in range(k_blocks):
                ks = slice(k * MXU_DIM, (k + 1) * MXU_DIM)
                lhs = h_sc[:, ks]
                n = half * k_blocks + k
                if anchors is not None and n % slices_per_chunk == 0 and n >= ANCHOR_LAG * slices_per_chunk:
                    lhs = _tie(lhs, anchors[n // slices_per_chunk - ANCHOR_LAG])
                part = jnp.dot(lhs, w_ref[ks, :], preferred_element_type=F32)
                acc = part if acc is None else acc + part
            u_scs[slot][half] = acc

    def gate(slot):
        u_sc = u_scs[slot]
        rows, anchors = [], []
        for c in range(FFN_ROW_CHUNKS):
            r0 = c * chunk
            cols = []
            for cc in range(FFN_COL_CHUNKS):
                cs = slice(cc * col_chunk, (cc + 1) * col_chunk)
                ys = []
                for half, (cw_ref, cb_ref) in enumerate(((cwg_ref, cbg_ref), (cwv_ref, cbv_ref))):
                    if c == 0:
                        prev = jnp.where(first, 0.0, carry_sc[half, f - 1, :, cs])
                    else:
                        prev = u_sc[half, r0 - CONV_HALO:r0, cs]
                    cur = u_sc[half, r0:r0 + chunk, cs]
                    e = jnp.concatenate([prev, cur], axis=0)
                    y = cb_ref[:, cs]
                    for tap in range(CONV_WIDTH - 1):
                        y = y + cw_ref[tap:tap + 1, cs] * pltpu.roll(e, CONV_WIDTH - 1 - tap, axis=0)[CONV_HALO:, :]
                    ys.append(y + cw_ref[CONV_WIDTH - 1:CONV_WIDTH, cs] * cur)
                yg, yv = ys
                act = yg / (1.0 + jnp.exp2(yg * -LOG2E)) * yv
                cols.append(act.astype(BF16))
                anchors.append(_zero_from(act))
            rows.append(jnp.concatenate(cols, axis=1))
        for half in range(2):
            carry_sc[half, f - 1] = u_sc[half, tm - CONV_HALO:, :]
        return jnp.concatenate(rows, axis=0), anchors

    def down(act):
        return jnp.dot(act, wd_ref[...], preferred_element_type=F32)

    @pl.when(f == 0)
    def _():
        h_sc[...] = _rms(x_ref[...], g_ref[...]).astype(BF16)
        up(0, None)

    @pl.when(f == 1)
    def _():
        act, anchors = gate(0)
        up(1, anchors)
        o_ref[...] = x_ref[...] + down(act)

    for parity in range(2):
        @pl.when((f >= 2) & (f < nf) & (f % 2 == parity))
        def _():
            act, anchors = gate(1 - parity)
            up(parity, anchors)
            o_ref[...] += down(act)

    @pl.when(f == nf)
    def _():
        act, _ = gate((nf - 1) % 2)
        o_ref[...] += down(act)


def _conv_ffn(x1, ffn_g, w_up, conv_w, conv_b, w_down, *, seq, tm, tf):
    tokens, d_model = x1.shape
    d_ff = w_down.shape[0]
    nf = d_ff // tf
    assert d_ff % tf == 0 and seq % tm == 0 and CONV_WIDTH - 1 <= CONV_HALO <= tm // FFN_ROW_CHUNKS
    assert d_model % MXU_DIM == 0 and (2 * d_model // MXU_DIM) % (FFN_ROW_CHUNKS * FFN_COL_CHUNKS) == 0 and nf >= 2
    kern = functools.partial(_ffn_kernel, tm=tm, tf=tf, d_model=d_model, tiles_per_seq=seq // tm, nf=nf)
    up_col = lambda f: jnp.minimum(f, nf - 1)
    down_col = lambda f: jnp.maximum(f - 1, 0)
    return pl.pallas_call(
        kern,
        grid=(tokens // tm, nf + 1),
        in_specs=[
            pl.BlockSpec((tm, d_model), lambda i, f: (i, 0)),
            pl.BlockSpec((1, d_model), lambda i, f: (0, 0)),
            pl.BlockSpec((d_model, tf), lambda i, f: (0, up_col(f))),
            pl.BlockSpec((d_model, tf), lambda i, f: (0, nf + up_col(f))),
            pl.BlockSpec((CONV_WIDTH, tf), lambda i, f: (0, down_col(f))),
            pl.BlockSpec((CONV_WIDTH, tf), lambda i, f: (0, nf + down_col(f))),
            pl.BlockSpec((1, tf), lambda i, f: (0, down_col(f))),
            pl.BlockSpec((1, tf), lambda i, f: (0, nf + down_col(f))),
            pl.BlockSpec((tf, d_model), lambda i, f: (down_col(f), 0)),
        ],
        out_specs=pl.BlockSpec((tm, d_model), lambda i, f: (i, 0)),
        out_shape=jax.ShapeDtypeStruct((tokens, d_model), F32),
        scratch_shapes=[pltpu.VMEM((tm, d_model), BF16),
                        pltpu.VMEM((2, tm, tf), F32),
                        pltpu.VMEM((2, tm, tf), F32),
                        pltpu.VMEM((2, nf, CONV_HALO, tf), F32)],
        compiler_params=pltpu.CompilerParams(
            dimension_semantics=("arbitrary", "arbitrary"), vmem_limit_bytes=VMEM_LIMIT),
        name="conv_ffn",
    )(x1, ffn_g, w_up, w_up, conv_w, conv_w, conv_b, conv_b, w_down)


def kernel(x, attn_norm_g, w_in, q_norm_g, k_norm_g, rel_bias, pool_w, pool_scale, w_out,
           ffn_norm_g, w_up, conv_w, conv_b, w_down):
    batch, seq, d_model = x.shape
    depth = w_in.shape[0]
    heads = rel_bias.shape[0]
    x2 = x.reshape(batch * seq, d_model)
    bias_tab, w_in_b = _bias_tables(rel_bias, w_in)
    for l in range(depth):
        qkm, vt, (w_out_l, w_up_l, w_down_l) = _in_proj(
            x2, attn_norm_g[l][None], w_in_b[l], q_norm_g[l][None], k_norm_g[l][None],
            pool_w[l].astype(BF16), pool_scale[l][None], (w_out, w_up, w_down), l, seq=seq, tm=1024)
        a = _moba_attention(rel_bias, qkm, vt, bias_tab, batch=batch, seq=seq, heads=heads)
        x1 = _out_proj(x2, a, qkm, w_out_l, tm=512)
        x2 = _conv_ffn(x1, ffn_norm_g[l][None], w_up_l, conv_w[l], conv_b[l][None], w_down_l,
                       seq=seq, tm=512, tf=512)
    return x2.reshape(batch, seq, d_model)
```

```python
import functools
import math

import jax
import jax.numpy as jnp
from jax import lax
from jax.experimental import pallas as pl
from jax.experimental.pallas import tpu as pltpu

HEAD_DIM = 128
MOBA_BLOCK = 256
MOBA_TOPK = 3
POOL_WINDOWS = (2, 4, 8, 16)
REL_BUCKETS = 32
REL_MAX_DIST = 1024
CONV_WIDTH = 3
EPS = 1e-6
NEG = -1e30

POOL_HALO = 16
CONV_HALO = 8
NEAR_DIST = 5

ONES_ROWS = 16
LOG2E = math.log2(math.e)
FAR_GROUP = 2

VMEM_LIMIT = 56 * 1024 * 1024

BF16 = jnp.bfloat16
F32 = jnp.float32
BF16_SUBLANES = 16
MXU_DIM = 256
INPROJ_TIE_SLICE = 4
FFN_ROW_CHUNKS = 8
FFN_COL_CHUNKS = 2
ANCHOR_LANES = 128
ANCHOR_LAG = 4


def _rel_bucket_thresholds():
    max_exact = REL_BUCKETS // 2
    span = REL_BUCKETS - max_exact
    out = []
    for k in range(1, span):
        edge = max_exact * (REL_MAX_DIST / max_exact) ** (k / span)
        r = round(edge)
        out.append(r if abs(edge - r) < 1e-9 else math.ceil(edge))
    return tuple(out)


def _bias_kernel(rb_ref, o_ref):
    h = pl.program_id(0)
    blk = MOBA_BLOCK
    shape = (8, 2 * blk)
    m = lax.broadcasted_iota(jnp.int32, shape, 1)
    offset = jnp.where(m < blk, m, m - 2 * blk)
    max_exact = REL_BUCKETS // 2
    for d in range(NEAR_DIST):
        n = jnp.maximum(d * blk + offset, 0)
        large = jnp.full(shape, max_exact, jnp.int32)
        for t in _rel_bucket_thresholds():
            large = large + (n >= t).astype(jnp.int32)
        bucket = jnp.where(n < max_exact, n, large)
        val = jnp.zeros(shape, F32)
        for b in range(REL_BUCKETS):
            val = jnp.where(bucket == b, rb_ref[h, b], val)
        val = val * LOG2E
        if d == 0:
            val = jnp.where(offset < 0, NEG, val)
        strip = jnp.broadcast_to(val[0:1, :], (blk, 2 * blk))
        o_ref[0, d] = pltpu.roll(strip, 0, axis=1, stride=1, stride_axis=0)[:, :blk]


def _bias_tables(rel_bias):
    assert (NEAR_DIST - 1) * MOBA_BLOCK + 1 >= _rel_bucket_thresholds()[-1]
    heads = rel_bias.shape[0]
    return pl.pallas_call(
        _bias_kernel,
        grid=(heads,),
        in_specs=[pl.BlockSpec(memory_space=pltpu.SMEM)],
        out_specs=pl.BlockSpec((1, NEAR_DIST, MOBA_BLOCK, MOBA_BLOCK), lambda h: (h, 0, 0, 0)),
        out_shape=jax.ShapeDtypeStruct((heads, NEAR_DIST, MOBA_BLOCK, MOBA_BLOCK), F32),
        name="bias_tables",
    )(rel_bias)


def _rms(y, g):
    ms = jnp.mean(y * y, axis=-1, keepdims=True)
    return y * lax.rsqrt(ms + EPS) * g


def _zero_from(vals):
    words = pltpu.bitcast(vals, jnp.uint32)
    tiles = [words[r:r + 8, l:l + ANCHOR_LANES] for r in range(0, words.shape[0], 8)
             for l in range(0, words.shape[1], ANCHOR_LANES)]
    while len(tiles) > 1:
        tiles = [a | b for a, b in zip(tiles[::2], tiles[1::2])] + ([tiles[-1]] if len(tiles) % 2 else [])
    return (tiles[0] >> 16) >> 16


def _tie(lhs, zero):
    top = pltpu.bitcast(lhs[:BF16_SUBLANES, :ANCHOR_LANES], jnp.uint32)
    top = pltpu.bitcast(top | zero, BF16)
    top = jnp.concatenate([top, lhs[:BF16_SUBLANES, ANCHOR_LANES:]], axis=1)
    return jnp.concatenate([top, lhs[BF16_SUBLANES:, :]], axis=0)


def _inproj_kernel(x_ref, g_ref, w_ref, qg_ref, kg_ref, pw_ref, ps_ref, *rest,
                   tm, d_model, tiles_per_seq, group_width, n_side):
    side_in, (o_ref, vt_ref), rest = rest[:n_side], rest[n_side:n_side + 2], rest[n_side + 2:]
    side_out, (h_sc, halo_sc) = rest[:n_side], rest[n_side:]
    i = pl.program_id(0)
    j = pl.program_id(1)
    k_blocks = d_model // MXU_DIM
    width = group_width
    n_chunks = w_ref.shape[1] // width

    for src, dst in zip(side_in, side_out):
        dst[...] = src[...].astype(BF16)

    def project(n, zero, scale=None):
        cols = slice(n * width, (n + 1) * width)
        halves = [slice(0, tm // 2), slice(tm // 2, tm)]
        accs = [None, None]
        for k in range(k_blocks):
            ks = slice(k * MXU_DIM, (k + 1) * MXU_DIM)
            if scale is None:
                lhs = h_sc[:, ks]
            else:
                lhs = (x_ref[:, ks] * scale * g_ref[:, ks]).astype(BF16)
                h_sc[:, ks] = lhs
            if zero is not None and k == INPROJ_TIE_SLICE:
                lhs = _tie(lhs, zero)
            for r, rows in enumerate(halves):
                part = jnp.dot(lhs[rows, :], w_ref[ks, cols], preferred_element_type=F32)
                accs[r] = part if accs[r] is None else accs[r] + part
        return jnp.concatenate(accs, axis=0)

    def chunked(finish, scale=None, order=None, after=None):
        zero, pending = None, None
        for pos, n in enumerate(order or range(n_chunks)):
            y = project(n, zero, scale if pos == 0 else None)
            if pending is not None:
                after(*pending)
            zero, state = finish(n, y)
            pending = (n, state) if after is not None else None
        if pending is not None:
            after(*pending)

    def head_norm(g):
        def finish(n, y):
            outs = [_rms(y[:, c:c + HEAD_DIM], g).astype(BF16) for c in range(0, width, HEAD_DIM)]
            out = jnp.concatenate(outs, axis=1)
            o_ref[:, n * width:(n + 1) * width] = out
            return _zero_from(out), None
        return finish

    @pl.when(j == 0)
    def _():
        x = x_ref[...]
        scale = lax.rsqrt(jnp.mean(x * x, axis=-1, keepdims=True) + EPS)
        chunked(head_norm(qg_ref[...] * (HEAD_DIM ** -0.5 * LOG2E)), scale)

    @pl.when(j == 1)
    def _():
        chunked(head_norm(kg_ref[...]))

    @pl.when(j == 2)
    def _():
        def finish(n, y):
            outs = [y[r:r + MOBA_BLOCK, :].T.astype(BF16) for r in range(0, tm, MOBA_BLOCK)]
            for r, out in enumerate(outs):
                vt_ref[r, n * width:(n + 1) * width, :] = out
            return _zero_from(jnp.concatenate(outs, axis=0)), None
        chunked(finish)

    @pl.when(j == 3)
    def _():
        seq_tile = i % tiles_per_seq
        t_top = seq_tile * tm + lax.broadcasted_iota(jnp.int32, (POOL_HALO, 1), 0)

        def finish(g, pg):
            sl = slice(g * width, (g + 1) * width)
            w = POOL_WINDOWS[g]
            halo = jnp.where(seq_tile == 0, 0.0, halo_sc[:, sl])
            halo_sc[:, sl] = pg[tm - POOL_HALO:, :]
            a = jnp.concatenate([halo, pg], axis=0)
            shift = 1
            while shift < w:
                a = a + pltpu.roll(a, shift, axis=0)
                shift *= 2
            sums = a[POOL_HALO:, :]
            count_top = jnp.minimum(t_top + 1, w).astype(F32)
            mean = jnp.concatenate([sums[:POOL_HALO, :] / count_top, sums[POOL_HALO:, :] / w], axis=0)
            mixed = (mean - pg).astype(BF16)
            return _zero_from(mixed), mixed

        def mix(g, mixed):
            sl = slice(g * width, (g + 1) * width)
            mg = jnp.dot(mixed, pw_ref[g], preferred_element_type=F32)
            o_ref[:, sl] = (mg * ps_ref[:, sl]).astype(BF16)

        chunked(finish, order=sorted(range(n_chunks), key=lambda g: -POOL_WINDOWS[g]), after=mix)


def _in_proj(x2, attn_g, w_in, q_g, k_g, pool_w, pool_scale, side, layer, *, seq, tm):
    tokens, d_model = x2.shape
    in_width = w_in.shape[1]
    tn = in_width // 4
    groups, group_width, _ = pool_w.shape
    assert groups == len(POOL_WINDOWS) and groups * group_width == tn
    assert seq % tm == 0 and tm >= 2 * POOL_HALO >= 2 * (max(POOL_WINDOWS) - 1) and tm % MOBA_BLOCK == 0
    assert group_width % HEAD_DIM == 0 and d_model % MXU_DIM == 0 and INPROJ_TIE_SLICE < d_model // MXU_DIM
    kern = functools.partial(_inproj_kernel, tm=tm, d_model=d_model, tiles_per_seq=seq // tm,
                             group_width=group_width, n_side=len(side))
    out_col = lambda i, j: (i, jnp.minimum(j, 1) + j // 3)
    steps = (tokens // tm) * 4
    slab_in, slab_out = [], []
    for a in side:
        rows = a.shape[1]
        hold = next(h for h in (1, 2, 4, 8) if steps % h == 0 and rows % (steps // h * BF16_SUBLANES) == 0)
        slab = (rows // (steps // hold), a.shape[2])
        slab_in.append(pl.BlockSpec((None,) + slab, lambda i, j, hold=hold: (layer, (i * 4 + j) // hold, 0)))
        slab_out.append(pl.BlockSpec(slab, lambda i, j, hold=hold: ((i * 4 + j) // hold, 0)))
    outs = pl.pallas_call(
        kern,
        grid=(tokens // tm, 4),
        in_specs=[
            pl.BlockSpec((tm, d_model), lambda i, j: (i, 0)),
            pl.BlockSpec((1, d_model), lambda i, j: (0, 0)),
            pl.BlockSpec((d_model, tn), lambda i, j: (0, j)),
            pl.BlockSpec((1, HEAD_DIM), lambda i, j: (0, 0)),
            pl.BlockSpec((1, HEAD_DIM), lambda i, j: (0, 0)),
            pl.BlockSpec((groups, group_width, group_width), lambda i, j: (0, 0, 0)),
            pl.BlockSpec((1, tn), lambda i, j: (0, 0)),
        ] + slab_in,
        out_specs=[pl.BlockSpec((tm, tn), out_col),
                   pl.BlockSpec((tm // MOBA_BLOCK, tn, MOBA_BLOCK), lambda i, j: (i, 0, 0))] + slab_out,
        out_shape=[jax.ShapeDtypeStruct((tokens, 3 * tn), BF16),
                   jax.ShapeDtypeStruct((tokens // MOBA_BLOCK, tn, MOBA_BLOCK), BF16)]
        + [jax.ShapeDtypeStruct(a.shape[1:], BF16) for a in side],
        scratch_shapes=[pltpu.VMEM((tm, d_model), BF16), pltpu.VMEM((POOL_HALO, tn), F32)],
        compiler_params=pltpu.CompilerParams(
            dimension_semantics=("arbitrary", "arbitrary"), vmem_limit_bytes=VMEM_LIMIT),
        name="in_proj",
    )(x2, attn_g, w_in, q_g, k_g, pool_w, pool_scale, *side)
    return outs[0], outs[1], outs[2:]


def _nt_dot(a, b):
    return lax.dot_general(a, b, (((1,), (1,)), ((), ())), preferred_element_type=F32)


def _attn_kernel(rb_ref, q_ref, k_ref, vt_ref, bias_ref, o_ref, kmean_sc, sel_sc, m_sc, acc_sc,
                 *, nb, heads):
    qb = pl.program_id(1)
    blk = MOBA_BLOCK

    @pl.when(qb == 0)
    def _():
        for n in range(nb):
            kn = k_ref[n * blk:(n + 1) * blk, :].astype(F32)
            kmean_sc[n:n + 1, :] = jnp.mean(kn, axis=0, keepdims=True)

    n_iota = lax.broadcasted_iota(jnp.int32, (nb, blk), 0)
    past = n_iota < qb
    start = pl.multiple_of(qb * blk, blk)
    head_slices = [slice(h * HEAD_DIM, (h + 1) * HEAD_DIM) for h in range(heads)]
    ones = jnp.ones((ONES_ROWS, blk), BF16)

    def values(n, hs):
        return jnp.concatenate([vt_ref[n, hs, :], ones], axis=0)


    prods = []
    for hs in head_slices:
        km = kmean_sc[:, hs]
        km_hi = km.astype(BF16)
        km_lo = (km - km_hi.astype(F32)).astype(BF16)
        lhs = jnp.concatenate([km_hi, km_lo, k_ref[pl.ds(start, blk), hs]], axis=0)
        prods.append(_nt_dot(lhs, q_ref[:, hs]))
    probs = []
    for h, hs in enumerate(head_slices):
        g = jnp.where(past, prods[h][:nb] + prods[h][nb:2 * nb], NEG)
        rank = jnp.zeros((nb, blk), jnp.int32)
        for m in range(nb):
            gm = g[m:m + 1, :]
            beats = (gm > g) | ((gm == g) & (n_iota > m))
            rank = rank + beats.astype(jnp.int32)
        sel_sc[h] = ((rank < MOBA_TOPK) & past).astype(F32)
        s = prods[h][2 * nb:] + bias_ref[h, 0]
        m0 = jnp.max(s, axis=0, keepdims=True)
        m_sc[h] = m0
        probs.append(jnp.exp2(s - m0).astype(BF16))
    for h, hs in enumerate(head_slices):
        acc_sc[h] = jnp.dot(values(qb, hs), probs[h], preferred_element_type=F32)

    def past_blocks(n0, count, dists):
        st = pl.multiple_of(n0 * blk, blk)
        prods = [_nt_dot(k_ref[pl.ds(st, count * blk), hs], q_ref[:, hs]) for hs in head_slices]
        scaled = []
        for h, s_all in enumerate(prods):
            far_bias = rb_ref[h, REL_BUCKETS - 1] * LOG2E
            tiles, m_tile = [], None
            for j, dist in enumerate(dists):
                s = s_all[j * blk:(j + 1) * blk]
                if dist is None:
                    c = far_bias
                else:
                    s = s + bias_ref[h, dist]
                    c = 0.0
                chosen = sel_sc[h, pl.ds(n0 + j, 1), :] > 0.5
                m_j = jnp.where(chosen, jnp.max(s, axis=0, keepdims=True) + c, NEG)
                m_tile = m_j if m_tile is None else jnp.maximum(m_tile, m_j)
                tiles.append((s, chosen, c))
            m_run = m_sc[h]
            m_new = jnp.maximum(m_run, m_tile)
            m_sc[h] = m_new
            p = [jnp.exp2(s - jnp.where(chosen, m_new - c, -NEG)).astype(BF16) for s, chosen, c in tiles]
            scaled.append((jnp.exp2(m_run - m_new), jnp.concatenate(p, axis=0)))
        for h, hs in enumerate(head_slices):
            alpha, p = scaled[h]
            vals = jnp.concatenate([values(n0 + j, hs) for j in range(count)], axis=1)
            acc_sc[h] = alpha * acc_sc[h] + jnp.dot(vals, p, preferred_element_type=F32)

    def far_group(i, carry):
        past_blocks(i * FAR_GROUP, FAR_GROUP, (None,) * FAR_GROUP)
        return carry

    def far_single(n, carry):
        past_blocks(n, 1, (None,))
        return carry

    def near_single(n, carry):
        past_blocks(n, 1, (qb - n,))
        return carry

    near_count = NEAR_DIST - 1
    far_end = jnp.maximum(qb - near_count, 0)
    far_groups = far_end // FAR_GROUP
    lax.fori_loop(0, far_groups, far_group, 0)
    lax.fori_loop(far_groups * FAR_GROUP, far_end, far_single, 0)
    lax.fori_loop(0, jnp.where(qb < near_count, qb, 0), near_single, 0)

    @pl.when(qb >= near_count)
    def _():
        past_blocks(qb - near_count, near_count, tuple(range(near_count, 0, -1)))

    for h, hs in enumerate(head_slices):
        acc = acc_sc[h]
        o_ref[:, hs] = (acc[:HEAD_DIM] / acc[HEAD_DIM:HEAD_DIM + 1]).T.astype(BF16)


def _moba_attention(rel_bias, qkm, vt, bias_tab, *, batch, seq, heads):
    tokens = qkm.shape[0]
    nb = seq // MOBA_BLOCK
    width = heads * HEAD_DIM
    assert seq % MOBA_BLOCK == 0
    kern = functools.partial(_attn_kernel, nb=nb, heads=heads)
    return pl.pallas_call(
        kern,
        grid=(batch, nb),
        in_specs=[
            pl.BlockSpec(memory_space=pltpu.SMEM),
            pl.BlockSpec((MOBA_BLOCK, width), lambda b, qb: (b * nb + qb, 0)),
            pl.BlockSpec((seq, width), lambda b, qb: (b, 1)),
            pl.BlockSpec((nb, width, MOBA_BLOCK), lambda b, qb: (b, 0, 0)),
            pl.BlockSpec((heads, NEAR_DIST, MOBA_BLOCK, MOBA_BLOCK), lambda b, qb: (0, 0, 0, 0),
                         pipeline_mode=pl.Buffered(1)),
        ],
        out_specs=pl.BlockSpec((MOBA_BLOCK, width), lambda b, qb: (b * nb + qb, 0)),
        out_shape=jax.ShapeDtypeStruct((tokens, width), BF16),
        scratch_shapes=[pltpu.VMEM((nb, width), F32),
                        pltpu.VMEM((heads, nb, MOBA_BLOCK), F32),
                        pltpu.VMEM((heads, 1, MOBA_BLOCK), F32),
                        pltpu.VMEM((heads, HEAD_DIM + ONES_ROWS, MOBA_BLOCK), F32)],
        compiler_params=pltpu.CompilerParams(
            dimension_semantics=("arbitrary", "arbitrary"), vmem_limit_bytes=VMEM_LIMIT),
        name="moba_attn",
    )(rel_bias, qkm, qkm, vt, bias_tab)


def _outproj_kernel(x_ref, a_ref, m_ref, wa_ref, wm_ref, o_ref):
    o_ref[...] = (x_ref[...]
                  + jnp.dot(a_ref[...], wa_ref[...], preferred_element_type=F32)
                  + jnp.dot(m_ref[...], wm_ref[...], preferred_element_type=F32))


def _out_proj(x2, a, qkm, w_out, *, tm):
    tokens, d_model = x2.shape
    half = a.shape[1]
    m_col = qkm.shape[1] // half - 1
    return pl.pallas_call(
        _outproj_kernel,
        grid=(tokens // tm,),
        in_specs=[
            pl.BlockSpec((tm, d_model), lambda i: (i, 0)),
            pl.BlockSpec((tm, half), lambda i: (i, 0)),
            pl.BlockSpec((tm, half), lambda i: (i, m_col)),
            pl.BlockSpec((half, d_model), lambda i: (0, 0)),
            pl.BlockSpec((half, d_model), lambda i: (1, 0)),
        ],
        out_specs=pl.BlockSpec((tm, d_model), lambda i: (i, 0)),
        out_shape=jax.ShapeDtypeStruct((tokens, d_model), F32),
        compiler_params=pltpu.CompilerParams(
            dimension_semantics=("arbitrary",), vmem_limit_bytes=VMEM_LIMIT),
        name="out_proj",
    )(x2, a, qkm, w_out, w_out)


def _ffn_kernel(x_ref, g_ref, wug_ref, wuv_ref, cwg_ref, cwv_ref, cbg_ref, cbv_ref, wd_ref,
                o_ref, h_sc, u0_sc, u1_sc, carry_sc, *, tm, tf, d_model, tiles_per_seq, nf):
    i = pl.program_id(0)
    f = pl.program_id(1)
    first = (i % tiles_per_seq) == 0
    u_scs = (u0_sc, u1_sc)
    k_blocks = d_model // MXU_DIM
    chunk = tm // FFN_ROW_CHUNKS
    col_chunk = tf // FFN_COL_CHUNKS
    n_chunks = FFN_ROW_CHUNKS * FFN_COL_CHUNKS
    slices_per_chunk = 2 * k_blocks // n_chunks

    def up(slot, anchors):
        for half, w_ref in enumerate((wug_ref, wuv_ref)):
            acc = None
            for k in range(k_blocks):
                ks = slice(k * MXU_DIM, (k + 1) * MXU_DIM)
                lhs = h_sc[:, ks]
                n = half * k_blocks + k
                if anchors is not None and n % slices_per_chunk == 0 and n >= ANCHOR_LAG * slices_per_chunk:
                    lhs = _tie(lhs, anchors[n // slices_per_chunk - ANCHOR_LAG])
                part = jnp.dot(lhs, w_ref[ks, :], preferred_element_type=F32)
                acc = part if acc is None else acc + part
            u_scs[slot][half] = acc

    def gate(slot):
        u_sc = u_scs[slot]
        rows, anchors = [], []
        for c in range(FFN_ROW_CHUNKS):
            r0 = c * chunk
            cols = []
            for cc in range(FFN_COL_CHUNKS):
                cs = slice(cc * col_chunk, (cc + 1) * col_chunk)
                ys = []
                for half, (cw_ref, cb_ref) in enumerate(((cwg_ref, cbg_ref), (cwv_ref, cbv_ref))):
                    if c == 0:
                        prev = jnp.where(first, 0.0, carry_sc[half, f - 1, :, cs])
                    else:
                        prev = u_sc[half, r0 - CONV_HALO:r0, cs]
                    cur = u_sc[half, r0:r0 + chunk, cs]
                    e = jnp.concatenate([prev, cur], axis=0)
                    y = cb_ref[:, cs]
                    for tap in range(CONV_WIDTH - 1):
                        y = y + cw_ref[tap:tap + 1, cs] * pltpu.roll(e, CONV_WIDTH - 1 - tap, axis=0)[CONV_HALO:, :]
                    ys.append(y + cw_ref[CONV_WIDTH - 1:CONV_WIDTH, cs] * cur)
                yg, yv = ys
                act = yg / (1.0 + jnp.exp2(yg * -LOG2E)) * yv
                cols.append(act.astype(BF16))
                anchors.append(_zero_from(act))
            rows.append(jnp.concatenate(cols, axis=1))
        for half in range(2):
            carry_sc[half, f - 1] = u_sc[half, tm - CONV_HALO:, :]
        return jnp.concatenate(rows, axis=0), anchors

    def down(act):
        return jnp.dot(act, wd_ref[...], preferred_element_type=F32)

    @pl.when(f == 0)
    def _():
        h_sc[...] = _rms(x_ref[...], g_ref[...]).astype(BF16)
        up(0, None)

    @pl.when(f == 1)
    def _():
        act, anchors = gate(0)
        up(1, anchors)
        o_ref[...] = x_ref[...] + down(act)

    for parity in range(2):
        @pl.when((f >= 2) & (f < nf) & (f % 2 == parity))
        def _():
            act, anchors = gate(1 - parity)
            up(parity, anchors)
            o_ref[...] += down(act)

    @pl.when(f == nf)
    def _():
        act, _ = gate((nf - 1) % 2)
        o_ref[...] += down(act)


def _conv_ffn(x1, ffn_g, w_up, conv_w, conv_b, w_down, *, seq, tm, tf):
    tokens, d_model = x1.shape
    d_ff = w_down.shape[0]
    nf = d_ff // tf
    assert d_ff % tf == 0 and seq % tm == 0 and CONV_WIDTH - 1 <= CONV_HALO <= tm // FFN_ROW_CHUNKS
    assert d_model % MXU_DIM == 0 and (2 * d_model // MXU_DIM) % (FFN_ROW_CHUNKS * FFN_COL_CHUNKS) == 0 and nf >= 2
    kern = functools.partial(_ffn_kernel, tm=tm, tf=tf, d_model=d_model, tiles_per_seq=seq // tm, nf=nf)
    up_col = lambda f: jnp.minimum(f, nf - 1)
    down_col = lambda f: jnp.maximum(f - 1, 0)
    return pl.pallas_call(
        kern,
        grid=(tokens // tm, nf + 1),
        in_specs=[
            pl.BlockSpec((tm, d_model), lambda i, f: (i, 0)),
            pl.BlockSpec((1, d_model), lambda i, f: (0, 0)),
            pl.BlockSpec((d_model, tf), lambda i, f: (0, up_col(f))),
            pl.BlockSpec((d_model, tf), lambda i, f: (0, nf + up_col(f))),
            pl.BlockSpec((CONV_WIDTH, tf), lambda i, f: (0, down_col(f))),
            pl.BlockSpec((CONV_WIDTH, tf), lambda i, f: (0, nf + down_col(f))),
            pl.BlockSpec((1, tf), lambda i, f: (0, down_col(f))),
            pl.BlockSpec((1, tf), lambda i, f: (0, nf + down_col(f))),
            pl.BlockSpec((tf, d_model), lambda i, f: (down_col(f), 0)),
        ],
        out_specs=pl.BlockSpec((tm, d_model), lambda i, f: (i, 0)),
        out_shape=jax.ShapeDtypeStruct((tokens, d_model), F32),
        scratch_shapes=[pltpu.VMEM((tm, d_model), BF16),
                        pltpu.VMEM((2, tm, tf), F32),
                        pltpu.VMEM((2, tm, tf), F32),
                        pltpu.VMEM((2, nf, CONV_HALO, tf), F32)],
        compiler_params=pltpu.CompilerParams(
            dimension_semantics=("arbitrary", "arbitrary"), vmem_limit_bytes=VMEM_LIMIT),
        name="conv_ffn",
    )(x1, ffn_g, w_up, w_up, conv_w, conv_w, conv_b, conv_b, w_down)


def kernel(x, attn_norm_g, w_in, q_norm_g, k_norm_g, rel_bias, pool_w, pool_scale, w_out,
           ffn_norm_g, w_up, conv_w, conv_b, w_down):
    batch, seq, d_model = x.shape
    depth = w_in.shape[0]
    heads = rel_bias.shape[0]
    x2 = x.reshape(batch * seq, d_model)
    bias_tab = _bias_tables(rel_bias)
    for l in range(depth):
        qkm, vt, (w_out_l, w_up_l, w_down_l) = _in_proj(
            x2, attn_norm_g[l][None], w_in[l].astype(BF16), q_norm_g[l][None], k_norm_g[l][None],
            pool_w[l].astype(BF16), pool_scale[l][None], (w_out, w_up, w_down), l, seq=seq, tm=1024)
        a = _moba_attention(rel_bias, qkm, vt, bias_tab, batch=batch, seq=seq, heads=heads)
        x1 = _out_proj(x2, a, qkm, w_out_l, tm=512)
        x2 = _conv_ffn(x1, ffn_norm_g[l][None], w_up_l, conv_w[l], conv_b[l][None], w_down_l,
                       seq=seq, tm=512, tf=512)
    return x2.reshape(batch, seq, d_model)
```

```python
import functools
import math

import jax
import jax.numpy as jnp
from jax import lax
from jax.experimental import pallas as pl
from jax.experimental.pallas import tpu as pltpu

HEAD_DIM = 128
MOBA_BLOCK = 256
MOBA_TOPK = 3
POOL_WINDOWS = (2, 4, 8, 16)
REL_BUCKETS = 32
REL_MAX_DIST = 1024
CONV_WIDTH = 3
EPS = 1e-6
NEG = -1e30

POOL_HALO = 16
CONV_HALO = 8
NEAR_DIST = 5

ONES_ROWS = 16
LOG2E = math.log2(math.e)
FAR_GROUP = 2
HEAD_PASS = 4

VMEM_LIMIT = 56 * 1024 * 1024

BF16 = jnp.bfloat16
F32 = jnp.float32
BF16_SUBLANES = 16
MXU_DIM = 256
INPROJ_TIE_SLICE = 4
FFN_ROW_CHUNKS = 8
FFN_COL_CHUNKS = 2
ANCHOR_LANES = 128
ANCHOR_LAG = 2


def _rel_bucket_thresholds():
    max_exact = REL_BUCKETS // 2
    span = REL_BUCKETS - max_exact
    out = []
    for k in range(1, span):
        edge = max_exact * (REL_MAX_DIST / max_exact) ** (k / span)
        r = round(edge)
        out.append(r if abs(edge - r) < 1e-9 else math.ceil(edge))
    return tuple(out)


def _bias_kernel(rb_ref, o_ref):
    h = pl.program_id(0)
    blk = MOBA_BLOCK
    shape = (8, 2 * blk)
    m = lax.broadcasted_iota(jnp.int32, shape, 1)
    offset = jnp.where(m < blk, m, m - 2 * blk)
    max_exact = REL_BUCKETS // 2
    for d in range(NEAR_DIST):
        n = jnp.maximum(d * blk + offset, 0)
        large = jnp.full(shape, max_exact, jnp.int32)
        for t in _rel_bucket_thresholds():
            large = large + (n >= t).astype(jnp.int32)
        bucket = jnp.where(n < max_exact, n, large)
        val = jnp.zeros(shape, F32)
        for b in range(REL_BUCKETS):
            val = jnp.where(bucket == b, rb_ref[h, b], val)
        val = val * LOG2E
        if d == 0:
            val = jnp.where(offset < 0, NEG, val)
        strip = jnp.broadcast_to(val[0:1, :], (blk, 2 * blk))
        o_ref[0, d] = pltpu.roll(strip, 0, axis=1, stride=1, stride_axis=0)[:, :blk]


def _bias_tables(rel_bias):
    assert (NEAR_DIST - 1) * MOBA_BLOCK + 1 >= _rel_bucket_thresholds()[-1]
    heads = rel_bias.shape[0]
    return pl.pallas_call(
        _bias_kernel,
        grid=(heads,),
        in_specs=[pl.BlockSpec(memory_space=pltpu.SMEM)],
        out_specs=pl.BlockSpec((1, NEAR_DIST, MOBA_BLOCK, MOBA_BLOCK), lambda h: (h, 0, 0, 0)),
        out_shape=jax.ShapeDtypeStruct((heads, NEAR_DIST, MOBA_BLOCK, MOBA_BLOCK), F32),
        name="bias_tables",
    )(rel_bias)


def _rms(y, g):
    ms = jnp.mean(y * y, axis=-1, keepdims=True)
    return y * lax.rsqrt(ms + EPS) * g


def _zero_from(vals):
    words = pltpu.bitcast(vals, jnp.uint32)
    tiles = [words[r:r + 8, l:l + ANCHOR_LANES] for r in range(0, words.shape[0], 8)
             for l in range(0, words.shape[1], ANCHOR_LANES)]
    while len(tiles) > 1:
        tiles = [a | b for a, b in zip(tiles[::2], tiles[1::2])] + ([tiles[-1]] if len(tiles) % 2 else [])
    return (tiles[0] >> 16) >> 16


def _tie(lhs, zero):
    top = pltpu.bitcast(lhs[:BF16_SUBLANES, :ANCHOR_LANES], jnp.uint32)
    top = pltpu.bitcast(top | zero, BF16)
    top = jnp.concatenate([top, lhs[:BF16_SUBLANES, ANCHOR_LANES:]], axis=1)
    return jnp.concatenate([top, lhs[BF16_SUBLANES:, :]], axis=0)


def _inproj_kernel(x_ref, g_ref, w_ref, qg_ref, kg_ref, pw_ref, ps_ref, *rest,
                   tm, d_model, tiles_per_seq, group_width, n_side):
    side_in, (o_ref, vt_ref), rest = rest[:n_side], rest[n_side:n_side + 2], rest[n_side + 2:]
    side_out, (h_sc, halo_sc) = rest[:n_side], rest[n_side:]
    i = pl.program_id(0)
    j = pl.program_id(1)
    k_blocks = d_model // MXU_DIM
    width = group_width
    n_chunks = w_ref.shape[1] // width

    for src, dst in zip(side_in, side_out):
        dst[...] = src[...].astype(BF16)

    def project(n, zero, scale=None):
        cols = slice(n * width, (n + 1) * width)
        halves = [slice(0, tm // 2), slice(tm // 2, tm)]
        accs = [None, None]
        for k in range(k_blocks):
            ks = slice(k * MXU_DIM, (k + 1) * MXU_DIM)
            if scale is None:
                lhs = h_sc[:, ks]
            else:
                lhs = (x_ref[:, ks] * scale * g_ref[:, ks]).astype(BF16)
                h_sc[:, ks] = lhs
            if zero is not None and k == INPROJ_TIE_SLICE:
                lhs = _tie(lhs, zero)
            for r, rows in enumerate(halves):
                part = jnp.dot(lhs[rows, :], w_ref[ks, cols], preferred_element_type=F32)
                accs[r] = part if accs[r] is None else accs[r] + part
        return jnp.concatenate(accs, axis=0)

    def chunked(finish, scale=None, order=None, after=None):
        zero, pending = None, None
        for pos, n in enumerate(order or range(n_chunks)):
            y = project(n, zero, scale if pos == 0 else None)
            if pending is not None:
                after(*pending)
            zero, state = finish(n, y)
            pending = (n, state) if after is not None else None
        if pending is not None:
            after(*pending)

    def head_norm(g):
        def finish(n, y):
            outs = [_rms(y[:, c:c + HEAD_DIM], g).astype(BF16) for c in range(0, width, HEAD_DIM)]
            out = jnp.concatenate(outs, axis=1)
            o_ref[:, n * width:(n + 1) * width] = out
            return _zero_from(out), None
        return finish

    @pl.when(j == 0)
    def _():
        x = x_ref[...]
        scale = lax.rsqrt(jnp.mean(x * x, axis=-1, keepdims=True) + EPS)
        chunked(head_norm(qg_ref[...] * (HEAD_DIM ** -0.5 * LOG2E)), scale)

    @pl.when(j == 1)
    def _():
        chunked(head_norm(kg_ref[...]))

    @pl.when(j == 2)
    def _():
        def finish(n, y):
            outs = [y[r:r + MOBA_BLOCK, :].T.astype(BF16) for r in range(0, tm, MOBA_BLOCK)]
            for r, out in enumerate(outs):
                vt_ref[r, n * width:(n + 1) * width, :] = out
            return _zero_from(jnp.concatenate(outs, axis=0)), None
        chunked(finish)

    @pl.when(j == 3)
    def _():
        seq_tile = i % tiles_per_seq
        t_top = seq_tile * tm + lax.broadcasted_iota(jnp.int32, (POOL_HALO, 1), 0)

        def finish(g, pg):
            sl = slice(g * width, (g + 1) * width)
            w = POOL_WINDOWS[g]
            halo = jnp.where(seq_tile == 0, 0.0, halo_sc[:, sl])
            halo_sc[:, sl] = pg[tm - POOL_HALO:, :]
            a = jnp.concatenate([halo, pg], axis=0)
            shift = 1
            while shift < w:
                a = a + pltpu.roll(a, shift, axis=0)
                shift *= 2
            sums = a[POOL_HALO:, :]
            count_top = jnp.minimum(t_top + 1, w).astype(F32)
            mean = jnp.concatenate([sums[:POOL_HALO, :] / count_top, sums[POOL_HALO:, :] / w], axis=0)
            mixed = (mean - pg).astype(BF16)
            return _zero_from(mixed), mixed

        def mix(g, mixed):
            sl = slice(g * width, (g + 1) * width)
            mg = jnp.dot(mixed, pw_ref[g], preferred_element_type=F32)
            o_ref[:, sl] = (mg * ps_ref[:, sl]).astype(BF16)

        chunked(finish, order=sorted(range(n_chunks), key=lambda g: -POOL_WINDOWS[g]), after=mix)


def _in_proj(x2, attn_g, w_in, q_g, k_g, pool_w, pool_scale, side, layer, *, seq, tm):
    tokens, d_model = x2.shape
    in_width = w_in.shape[1]
    tn = in_width // 4
    groups, group_width, _ = pool_w.shape
    assert groups == len(POOL_WINDOWS) and groups * group_width == tn
    assert seq % tm == 0 and tm >= 2 * POOL_HALO >= 2 * (max(POOL_WINDOWS) - 1) and tm % MOBA_BLOCK == 0
    assert group_width % HEAD_DIM == 0 and d_model % MXU_DIM == 0 and INPROJ_TIE_SLICE < d_model // MXU_DIM
    kern = functools.partial(_inproj_kernel, tm=tm, d_model=d_model, tiles_per_seq=seq // tm,
                             group_width=group_width, n_side=len(side))
    out_col = lambda i, j: (i, jnp.minimum(j, 1) + j // 3)
    steps = (tokens // tm) * 4
    slab_in, slab_out = [], []
    for a in side:
        rows = a.shape[1]
        hold = next(h for h in (1, 2, 4, 8) if steps % h == 0 and rows % (steps // h * BF16_SUBLANES) == 0)
        slab = (rows // (steps // hold), a.shape[2])
        slab_in.append(pl.BlockSpec((None,) + slab, lambda i, j, hold=hold: (layer, (i * 4 + j) // hold, 0)))
        slab_out.append(pl.BlockSpec(slab, lambda i, j, hold=hold: ((i * 4 + j) // hold, 0)))
    outs = pl.pallas_call(
        kern,
        grid=(tokens // tm, 4),
        in_specs=[
            pl.BlockSpec((tm, d_model), lambda i, j: (i, 0)),
            pl.BlockSpec((1, d_model), lambda i, j: (0, 0)),
            pl.BlockSpec((d_model, tn), lambda i, j: (0, j)),
            pl.BlockSpec((1, HEAD_DIM), lambda i, j: (0, 0)),
            pl.BlockSpec((1, HEAD_DIM), lambda i, j: (0, 0)),
            pl.BlockSpec((groups, group_width, group_width), lambda i, j: (0, 0, 0)),
            pl.BlockSpec((1, tn), lambda i, j: (0, 0)),
        ] + slab_in,
        out_specs=[pl.BlockSpec((tm, tn), out_col),
                   pl.BlockSpec((tm // MOBA_BLOCK, tn, MOBA_BLOCK), lambda i, j: (i, 0, 0))] + slab_out,
        out_shape=[jax.ShapeDtypeStruct((tokens, 3 * tn), BF16),
                   jax.ShapeDtypeStruct((tokens // MOBA_BLOCK, tn, MOBA_BLOCK), BF16)]
        + [jax.ShapeDtypeStruct(a.shape[1:], BF16) for a in side],
        scratch_shapes=[pltpu.VMEM((tm, d_model), BF16), pltpu.VMEM((POOL_HALO, tn), F32)],
        compiler_params=pltpu.CompilerParams(
            dimension_semantics=("arbitrary", "arbitrary"), vmem_limit_bytes=VMEM_LIMIT),
        name="in_proj",
    )(x2, attn_g, w_in, q_g, k_g, pool_w, pool_scale, *side)
    return outs[0], outs[1], outs[2:]


def _nt_dot(a, b):
    return lax.dot_general(a, b, (((1,), (1,)), ((), ())), preferred_element_type=F32)


def _attn_kernel(rb_ref, q_ref, k_ref, vt_ref, bias_ref, o_ref, kmean_sc, sel_sc, m_sc, acc_sc,
                 *, nb, heads):
    qb = pl.program_id(1)
    blk = MOBA_BLOCK

    @pl.when(qb == 0)
    def _():
        for n in range(nb):
            kn = k_ref[n * blk:(n + 1) * blk, :].astype(F32)
            kmean_sc[n:n + 1, :] = jnp.mean(kn, axis=0, keepdims=True)

    n_iota = lax.broadcasted_iota(jnp.int32, (nb, blk), 0)
    past = n_iota < qb
    start = pl.multiple_of(qb * blk, blk)
    head_slices = [slice(h * HEAD_DIM, (h + 1) * HEAD_DIM) for h in range(heads)]
    ones = jnp.ones((ONES_ROWS, blk), BF16)

    def values(n, hs):
        return jnp.concatenate([vt_ref[n, hs, :], ones], axis=0)


    prods = []
    for hs in head_slices:
        km = kmean_sc[:, hs]
        km_hi = km.astype(BF16)
        km_lo = (km - km_hi.astype(F32)).astype(BF16)
        lhs = jnp.concatenate([km_hi, km_lo, k_ref[pl.ds(start, blk), hs]], axis=0)
        prods.append(_nt_dot(lhs, q_ref[:, hs]))
    probs = []
    for h, hs in enumerate(head_slices):
        g = jnp.where(past, prods[h][:nb] + prods[h][nb:2 * nb], NEG)
        rank = jnp.zeros((nb, blk), jnp.int32)
        for m in range(nb):
            gm = g[m:m + 1, :]
            beats = (gm > g) | ((gm == g) & (n_iota > m))
            rank = rank + beats.astype(jnp.int32)
        sel_sc[h] = ((rank < MOBA_TOPK) & past).astype(F32)
        s = prods[h][2 * nb:] + bias_ref[h, 0]
        m0 = jnp.max(s, axis=0, keepdims=True)
        m_sc[h] = m0
        probs.append(jnp.exp2(s - m0).astype(BF16))
    for h, hs in enumerate(head_slices):
        acc_sc[h] = jnp.dot(values(qb, hs), probs[h], preferred_element_type=F32)

    def past_blocks(n0, count, dists):
        st = pl.multiple_of(n0 * blk, blk)
        for lo in range(0, heads, HEAD_PASS):
            head_pass(st, n0, count, dists, range(lo, lo + HEAD_PASS))

    def head_pass(st, n0, count, dists, hs_idx):
        prods = {h: _nt_dot(k_ref[pl.ds(st, count * blk), head_slices[h]], q_ref[:, head_slices[h]])
                 for h in hs_idx}
        scaled = {}
        for h, s_all in prods.items():
            far_bias = rb_ref[h, REL_BUCKETS - 1] * LOG2E
            tiles, m_tile = [], None
            for j, dist in enumerate(dists):
                s = s_all[j * blk:(j + 1) * blk]
                if dist is None:
                    c = far_bias
                else:
                    s = s + bias_ref[h, dist]
                    c = 0.0
                chosen = sel_sc[h, pl.ds(n0 + j, 1), :] > 0.5
                m_j = jnp.where(chosen, jnp.max(s, axis=0, keepdims=True) + c, NEG)
                m_tile = m_j if m_tile is None else jnp.maximum(m_tile, m_j)
                tiles.append((s, chosen, c))
            m_run = m_sc[h]
            m_new = jnp.maximum(m_run, m_tile)
            m_sc[h] = m_new
            p = [jnp.exp2(s - jnp.where(chosen, m_new - c, -NEG)).astype(BF16) for s, chosen, c in tiles]
            scaled[h] = (jnp.exp2(m_run - m_new), jnp.concatenate(p, axis=0))
        for h, (alpha, p) in scaled.items():
            hs = head_slices[h]
            vals = jnp.concatenate([values(n0 + j, hs) for j in range(count)], axis=1)
            acc_sc[h] = alpha * acc_sc[h] + jnp.dot(vals, p, preferred_element_type=F32)

    def far_group(i, carry):
        past_blocks(i * FAR_GROUP, FAR_GROUP, (None,) * FAR_GROUP)
        return carry

    def far_single(n, carry):
        past_blocks(n, 1, (None,))
        return carry

    def near_single(n, carry):
        past_blocks(n, 1, (qb - n,))
        return carry

    near_count = NEAR_DIST - 1
    far_end = jnp.maximum(qb - near_count, 0)
    far_groups = far_end // FAR_GROUP
    lax.fori_loop(0, far_groups, far_group, 0)
    lax.fori_loop(far_groups * FAR_GROUP, far_end, far_single, 0)
    lax.fori_loop(0, jnp.where(qb < near_count, qb, 0), near_single, 0)

    @pl.when(qb >= near_count)
    def _():
        past_blocks(qb - near_count, near_count, tuple(range(near_count, 0, -1)))

    for h, hs in enumerate(head_slices):
        acc = acc_sc[h]
        o_ref[:, hs] = (acc[:HEAD_DIM] / acc[HEAD_DIM:HEAD_DIM + 1]).T.astype(BF16)


def _moba_attention(rel_bias, qkm, vt, bias_tab, *, batch, seq, heads):
    tokens = qkm.shape[0]
    nb = seq // MOBA_BLOCK
    width = heads * HEAD_DIM
    assert seq % MOBA_BLOCK == 0
    kern = functools.partial(_attn_kernel, nb=nb, heads=heads)
    return pl.pallas_call(
        kern,
        grid=(batch, nb),
        in_specs=[
            pl.BlockSpec(memory_space=pltpu.SMEM),
            pl.BlockSpec((MOBA_BLOCK, width), lambda b, qb: (b * nb + qb, 0)),
            pl.BlockSpec((seq, width), lambda b, qb: (b, 1)),
            pl.BlockSpec((nb, width, MOBA_BLOCK), lambda b, qb: (b, 0, 0)),
            pl.BlockSpec((heads, NEAR_DIST, MOBA_BLOCK, MOBA_BLOCK), lambda b, qb: (0, 0, 0, 0),
                         pipeline_mode=pl.Buffered(1)),
        ],
        out_specs=pl.BlockSpec((MOBA_BLOCK, width), lambda b, qb: (b * nb + qb, 0)),
        out_shape=jax.ShapeDtypeStruct((tokens, width), BF16),
        scratch_shapes=[pltpu.VMEM((nb, width), F32),
                        pltpu.VMEM((heads, nb, MOBA_BLOCK), F32),
                        pltpu.VMEM((heads, 1, MOBA_BLOCK), F32),
                        pltpu.VMEM((heads, HEAD_DIM + ONES_ROWS, MOBA_BLOCK), F32)],
        compiler_params=pltpu.CompilerParams(
            dimension_semantics=("arbitrary", "arbitrary"), vmem_limit_bytes=VMEM_LIMIT),
        name="moba_attn",
    )(rel_bias, qkm, qkm, vt, bias_tab)


def _outproj_kernel(x_ref, a_ref, m_ref, wa_ref, wm_ref, o_ref):
    o_ref[...] = (x_ref[...]
                  + jnp.dot(a_ref[...], wa_ref[...], preferred_element_type=F32)
                  + jnp.dot(m_ref[...], wm_ref[...], preferred_element_type=F32))


def _out_proj(x2, a, qkm, w_out, *, tm):
    tokens, d_model = x2.shape
    half = a.shape[1]
    m_col = qkm.shape[1] // half - 1
    return pl.pallas_call(
        _outproj_kernel,
        grid=(tokens // tm,),
        in_specs=[
            pl.BlockSpec((tm, d_model), lambda i: (i, 0)),
            pl.BlockSpec((tm, half), lambda i: (i, 0)),
            pl.BlockSpec((tm, half), lambda i: (i, m_col)),
            pl.BlockSpec((half, d_model), lambda i: (0, 0)),
            pl.BlockSpec((half, d_model), lambda i: (1, 0)),
        ],
        out_specs=pl.BlockSpec((tm, d_model), lambda i: (i, 0)),
        out_shape=jax.ShapeDtypeStruct((tokens, d_model), F32),
        compiler_params=pltpu.CompilerParams(
            dimension_semantics=("arbitrary",), vmem_limit_bytes=VMEM_LIMIT),
        name="out_proj",
    )(x2, a, qkm, w_out, w_out)


def _ffn_kernel(x_ref, g_ref, wug_ref, wuv_ref, cwg_ref, cwv_ref, cbg_ref, cbv_ref, wd_ref,
                o_ref, h_sc, u0_sc, u1_sc, carry_sc, *, tm, tf, d_model, tiles_per_seq, nf):
    i = pl.program_id(0)
    f = pl.program_id(1)
    first = (i % tiles_per_seq) == 0
    u_scs = (u0_sc, u1_sc)
    k_blocks = d_model // MXU_DIM
    chunk = tm // FFN_ROW_CHUNKS
    col_chunk = tf // FFN_COL_CHUNKS
    n_chunks = FFN_ROW_CHUNKS * FFN_COL_CHUNKS
    slices_per_chunk = 2 * k_blocks // n_chunks

    def up(slot, anchors):
        for half, w_ref in enumerate((wug_ref, wuv_ref)):
            acc = None
            for k in range(k_blocks):
                ks = slice(k * MXU_DIM, (k + 1) * MXU_DIM)
                lhs = h_sc[:, ks]
                n = half * k_blocks + k
                if anchors is not None and n % slices_per_chunk == 0 and n >= ANCHOR_LAG * slices_per_chunk:
                    lhs = _tie(lhs, anchors[n // slices_per_chunk - ANCHOR_LAG])
                part = jnp.dot(lhs, w_ref[ks, :], preferred_element_type=F32)
                acc = part if acc is None else acc + part
            u_scs[slot][half] = acc

    def gate(slot):
        u_sc = u_scs[slot]
        rows, anchors = [], []
        for c in range(FFN_ROW_CHUNKS):
            r0 = c * chunk
            cols = []
            for cc in range(FFN_COL_CHUNKS):
                cs = slice(cc * col_chunk, (cc + 1) * col_chunk)
                ys = []
                for half, (cw_ref, cb_ref) in enumerate(((cwg_ref, cbg_ref), (cwv_ref, cbv_ref))):
                    if c == 0:
                        prev = jnp.where(first, 0.0, carry_sc[half, f - 1, :, cs])
                    else:
                        prev = u_sc[half, r0 - CONV_HALO:r0, cs]
                    cur = u_sc[half, r0:r0 + chunk, cs]
                    e = jnp.concatenate([prev, cur], axis=0)
                    y = cb_ref[:, cs]
                    for tap in range(CONV_WIDTH - 1):
                        y = y + cw_ref[tap:tap + 1, cs] * pltpu.roll(e, CONV_WIDTH - 1 - tap, axis=0)[CONV_HALO:, :]
                    ys.append(y + cw_ref[CONV_WIDTH - 1:CONV_WIDTH, cs] * cur)
                yg, yv = ys
                act = yg / (1.0 + jnp.exp2(yg * -LOG2E)) * yv
                cols.append(act.astype(BF16))
                anchors.append(_zero_from(act))
            rows.append(jnp.concatenate(cols, axis=1))
        for half in range(2):
            carry_sc[half, f - 1] = u_sc[half, tm - CONV_HALO:, :]
        return jnp.concatenate(rows, axis=0), anchors

    def down(act):
        return jnp.dot(act, wd_ref[...], preferred_element_type=F32)

    @pl.when(f == 0)
    def _():
        h_sc[...] = _rms(x_ref[...], g_ref[...]).astype(BF16)
        up(0, None)

    @pl.when(f == 1)
    def _():
        act, anchors = gate(0)
        up(1, anchors)
        o_ref[...] = x_ref[...] + down(act)

    for parity in range(2):
        @pl.when((f >= 2) & (f < nf) & (f % 2 == parity))
        def _():
            act, anchors = gate(1 - parity)
            up(parity, anchors)
            o_ref[...] += down(act)

    @pl.when(f == nf)
    def _():
        act, _ = gate((nf - 1) % 2)
        o_ref[...] += down(act)


def _conv_ffn(x1, ffn_g, w_up, conv_w, conv_b, w_down, *, seq, tm, tf):
    tokens, d_model = x1.shape
    d_ff = w_down.shape[0]
    nf = d_ff // tf
    assert d_ff % tf == 0 and seq % tm == 0 and CONV_WIDTH - 1 <= CONV_HALO <= tm // FFN_ROW_CHUNKS
    assert d_model % MXU_DIM == 0 and (2 * d_model // MXU_DIM) % (FFN_ROW_CHUNKS * FFN_COL_CHUNKS) == 0 and nf >= 2
    kern = functools.partial(_ffn_kernel, tm=tm, tf=tf, d_model=d_model, tiles_per_seq=seq // tm, nf=nf)
    up_col = lambda f: jnp.minimum(f, nf - 1)
    down_col = lambda f: jnp.maximum(f - 1, 0)
    return pl.pallas_call(
        kern,
        grid=(tokens // tm, nf + 1),
        in_specs=[
            pl.BlockSpec((tm, d_model), lambda i, f: (i, 0)),
            pl.BlockSpec((1, d_model), lambda i, f: (0, 0)),
            pl.BlockSpec((d_model, tf), lambda i, f: (0, up_col(f))),
            pl.BlockSpec((d_model, tf), lambda i, f: (0, nf + up_col(f))),
            pl.BlockSpec((CONV_WIDTH, tf), lambda i, f: (0, down_col(f))),
            pl.BlockSpec((CONV_WIDTH, tf), lambda i, f: (0, nf + down_col(f))),
            pl.BlockSpec((1, tf), lambda i, f: (0, down_col(f))),
            pl.BlockSpec((1, tf), lambda i, f: (0, nf + down_col(f))),
            pl.BlockSpec((tf, d_model), lambda i, f: (down_col(f), 0)),
        ],
        out_specs=pl.BlockSpec((tm, d_model), lambda i, f: (i, 0)),
        out_shape=jax.ShapeDtypeStruct((tokens, d_model), F32),
        scratch_shapes=[pltpu.VMEM((tm, d_model), BF16),
                        pltpu.VMEM((2, tm, tf), F32),
                        pltpu.VMEM((2, tm, tf), F32),
                        pltpu.VMEM((2, nf, CONV_HALO, tf), F32)],
        compiler_params=pltpu.CompilerParams(
            dimension_semantics=("arbitrary", "arbitrary"), vmem_limit_bytes=VMEM_LIMIT),
        name="conv_ffn",
    )(x1, ffn_g, w_up, w_up, conv_w, conv_w, conv_b, conv_b, w_down)


def kernel(x, attn_norm_g, w_in, q_norm_g, k_norm_g, rel_bias, pool_w, pool_scale, w_out,
           ffn_norm_g, w_up, conv_w, conv_b, w_down):
    batch, seq, d_model = x.shape
    depth = w_in.shape[0]
    heads = rel_bias.shape[0]
    x2 = x.reshape(batch * seq, d_model)
    bias_tab = _bias_tables(rel_bias)
    for l in range(depth):
        qkm, vt, (w_out_l, w_up_l, w_down_l) = _in_proj(
            x2, attn_norm_g[l][None], w_in[l].astype(BF16), q_norm_g[l][None], k_norm_g[l][None],
            pool_w[l].astype(BF16), pool_scale[l][None], (w_out, w_up, w_down), l, seq=seq, tm=1024)
        a = _moba_attention(rel_bias, qkm, vt, bias_tab, batch=batch, seq=seq, heads=heads)
        x1 = _out_proj(x2, a, qkm, w_out_l, tm=512)
        x2 = _conv_ffn(x1, ffn_norm_g[l][None], w_up_l, conv_w[l], conv_b[l][None], w_down_l,
                       seq=seq, tm=512, tf=512)
    return x2.reshape(batch, seq, d_model)
```
